```python
import jax, jax.numpy as jnp
from jax import lax
import numpy as np

D_MODEL = 1024
BATCH = 8
SEQ = 4096
DEPTH = 4

MIX_WIDTH = D_MODEL
RWKV_WIDTH = MIX_WIDTH // 2
RWKV_HEAD = 64
RWKV_HEADS = RWKV_WIDTH // RWKV_HEAD
DECAY_LORA = 64
ICLR_LORA = 64
VMIX_LORA = 32
MLA_WIDTH = MIX_WIDTH - RWKV_WIDTH
MLA_V_HEAD = 64
MLA_HEADS = MLA_WIDTH // MLA_V_HEAD
MLA_NOPE = 64
MLA_ROPE = 32
Q_LORA = 384
KV_LORA = 256
ROPE_THETA = 10000.0
Q_BLOCK = 128
NORM_EPS = 1e-6
GN_EPS = 64e-5

SHIFT_WIDTH = 3 * RWKV_WIDTH + DECAY_LORA + ICLR_LORA
O_GR = SHIFT_WIDTH
O_CQ = O_GR + RWKV_WIDTH
O_CKV = O_CQ + Q_LORA
O_KR = O_CKV + KV_LORA
O_GM = O_KR + MLA_ROPE
IN_WIDTH = O_GM + MLA_WIDTH

kernel_name = 'hymba_rwkv7_mla_adaln_block'


def rms_norm(x, g, eps=NORM_EPS):
    xf = x.astype(jnp.float32)
    y = xf * lax.rsqrt(jnp.mean(xf * xf, axis=-1, keepdims=True) + eps)
    return (y * g.astype(jnp.float32)).astype(x.dtype)


def token_shift_lerp(p, mu):
    prev = jnp.pad(p[:, :-1], ((0, 0), (1, 0), (0, 0)))
    return p + (prev - p) * mu


def rope_tables(positions):
    inv = ROPE_THETA ** (-jnp.arange(0, MLA_ROPE, 2, dtype=jnp.float32) / MLA_ROPE)
    ang = positions.astype(jnp.float32)[..., None] * inv
    ang = jnp.concatenate([ang, ang], axis=-1)
    return jnp.cos(ang), jnp.sin(ang)


def apply_rope(x, cos, sin):
    x1, x2 = jnp.split(x, 2, axis=-1)
    rot = jnp.concatenate([-x2, x1], axis=-1)
    return (x.astype(jnp.float32) * cos + rot.astype(jnp.float32) * sin).astype(x.dtype)


def rwkv7_scan(r, w, k, v, kk, a):
    B, S, H, N = r.shape

    def step(state, inp):
        r_t, w_t, k_t, v_t, kk_t, a_t = inp
        sa = jnp.einsum('bhvk,bhk->bhv', state, kk_t)
        state = (state * w_t[:, :, None, :]
                 - sa[..., None] * (kk_t * a_t)[:, :, None, :]
                 + v_t[..., None] * k_t[:, :, None, :])
        y = jnp.einsum('bhvk,bhk->bhv', state, r_t)
        return state, y

    xs = tuple(jnp.moveaxis(t, 1, 0) for t in (r, w, k, v, kk, a))
    s0 = jnp.zeros((B, H, N, N), jnp.float32)
    _, ys = lax.scan(step, s0, xs)
    return jnp.moveaxis(ys, 0, 1)


def group_norm_heads(y, w, b):
    mean = jnp.mean(y, axis=-1, keepdims=True)
    var = jnp.mean(jnp.square(y - mean), axis=-1, keepdims=True)
    yn = (y - mean) * lax.rsqrt(var + GN_EPS)
    return (yn * w.reshape(RWKV_HEADS, RWKV_HEAD).astype(jnp.float32)
            + b.reshape(RWKV_HEADS, RWKV_HEAD).astype(jnp.float32))


def rwkv7_time_mix(p, p_vmix, v_first, mu, mu_v, w0, w_dec_up, a0, w_icl_up, v0, w_vmix_up,
                   k_k, k_a, r_k, lnx_w, lnx_b):
    B, S, _ = p.shape
    H, N = RWKV_HEADS, RWKV_HEAD
    ps = token_shift_lerp(p, mu)
    r, k, v, w_lo, a_lo = jnp.split(
        ps, [RWKV_WIDTH, 2 * RWKV_WIDTH, 3 * RWKV_WIDTH, 3 * RWKV_WIDTH + DECAY_LORA], axis=-1)
    w_log = -jax.nn.softplus(-(w0 + jnp.tanh(w_lo) @ w_dec_up)) - 0.5
    decay = jnp.exp(-jnp.exp(w_log.astype(jnp.float32)))
    a = jax.nn.sigmoid(a0 + a_lo @ w_icl_up)
    if v_first is None:
        v_first = v
    else:
        v_lo = token_shift_lerp(p_vmix, mu_v)
        v = v + (v_first - v) * jax.nn.sigmoid(v0 + v_lo @ w_vmix_up)

    def heads(t):
        return t.reshape(B, S, H, N).astype(jnp.float32)

    kk = heads(k * k_k)
    kk = kk / jnp.maximum(jnp.sqrt(jnp.sum(kk * kk, axis=-1, keepdims=True)), 1e-12)
    k = k * (1 + (a - 1) * k_a)
    rh, kh, vh, ah = heads(r), heads(k), heads(v), heads(a)
    y = rwkv7_scan(rh, heads(decay), kh, vh, kk, ah)
    y = group_norm_heads(y, lnx_w, lnx_b)
    y = y + jnp.sum(rh * kh * r_k.astype(jnp.float32), axis=-1, keepdims=True) * vh
    return y.reshape(B, S, RWKV_WIDTH).astype(p.dtype), v_first


def causal_block_attention(q_nope, q_rope, k_nope, k_rope, v):
    B, S, H, _ = q_nope.shape
    nb = S // Q_BLOCK
    scale = (MLA_NOPE + MLA_ROPE) ** -0.5
    qn = q_nope.reshape(B, nb, Q_BLOCK, H, MLA_NOPE).transpose(1, 0, 2, 3, 4)
    qr = q_rope.reshape(B, nb, Q_BLOCK, H, MLA_ROPE).transpose(1, 0, 2, 3, 4)
    key_idx = jnp.arange(S)

    def block(args):
        qn_b, qr_b, start = args
        s = (jnp.einsum('bqhd,bkhd->bhqk', qn_b, k_nope)
             + jnp.einsum('bqhd,bkd->bhqk', qr_b, k_rope)).astype(jnp.float32) * scale
        q_idx = start + jnp.arange(Q_BLOCK)
        s = jnp.where(key_idx[None, :] <= q_idx[:, None], s, -jnp.inf)
        prob = jax.nn.softmax(s, axis=-1).astype(v.dtype)
        return jnp.einsum('bhqk,bkhd->bqhd', prob, v)

    starts = jnp.arange(nb) * Q_BLOCK
    out = lax.map(block, (qn, qr, starts))
    return out.transpose(1, 0, 2, 3, 4).reshape(B, S, H, MLA_V_HEAD)


def mla_branch(c_q, c_kv, k_rope_in, cos, sin, q_norm_g, kv_norm_g, w_uq, w_ukv):
    B, S, _ = c_q.shape
    H = MLA_HEADS
    q = (rms_norm(c_q, q_norm_g) @ w_uq).reshape(B, S, H, MLA_NOPE + MLA_ROPE)
    q_nope = q[..., :MLA_NOPE]
    q_rope = apply_rope(q[..., MLA_NOPE:], cos[:, :, None, :], sin[:, :, None, :])
    kv = (rms_norm(c_kv, kv_norm_g) @ w_ukv).reshape(B, S, H, MLA_NOPE + MLA_V_HEAD)
    k_nope, v = kv[..., :MLA_NOPE], kv[..., MLA_NOPE:]
    k_rope = apply_rope(k_rope_in, cos, sin)
    y = causal_block_attention(q_nope, q_rope, k_nope, k_rope, v)
    return y.reshape(B, S, MLA_WIDTH)


def setup_inputs(seed: int = 0) -> dict:
    key = jax.random.key(seed)
    ks = iter(jax.random.split(key, 40))
    L, D, Lv = DEPTH, D_MODEL, DEPTH - 1

    def nrm(shape, s):
        return jax.random.normal(next(ks), shape, jnp.float32) * s

    def uni(shape, lo, hi):
        return jax.random.uniform(next(ks), shape, jnp.float32, lo, hi)

    return {
        'x': nrm((BATCH, SEQ, D), 1.0),
        'c': nrm((BATCH, D), 1.0),
        'positions': jnp.broadcast_to(jnp.arange(SEQ, dtype=jnp.int32), (BATCH, SEQ)),
        'norm_g': 1.0 + nrm((L, D), 0.02),
        'w_ada': nrm((L, D, 3 * D), 0.5 * D ** -0.5),
        'b_ada': nrm((L, 3 * D), 0.01),
        'w_in': nrm((L, D, IN_WIDTH), D ** -0.5),
        'w_vmix_down': nrm((Lv, D, VMIX_LORA), D ** -0.5),
        'mu_shift': uni((L, SHIFT_WIDTH), 0.0, 1.0),
        'mu_vmix': uni((Lv, VMIX_LORA), 0.0, 1.0),
        'w0': uni((L, RWKV_WIDTH), -6.5, -1.5),
        'w_decay_up': nrm((L, DECAY_LORA, RWKV_WIDTH), 0.1 * DECAY_LORA ** -0.5),
        'a0': nrm((L, RWKV_WIDTH), 0.1),
        'w_iclr_up': nrm((L, ICLR_LORA, RWKV_WIDTH), 0.3 * ICLR_LORA ** -0.5),
        'v0': 1.0 + nrm((Lv, RWKV_WIDTH), 0.1),
        'w_vmix_up': nrm((Lv, VMIX_LORA, RWKV_WIDTH), 0.3 * VMIX_LORA ** -0.5),
        'k_k': 0.85 + nrm((L, RWKV_WIDTH), 0.02),
        'k_a': 1.0 + nrm((L, RWKV_WIDTH), 0.02),
        'r_k': nrm((L, RWKV_HEADS, RWKV_HEAD), 0.1),
        'lnx_w': 1.0 + nrm((L, RWKV_WIDTH), 0.02),
        'lnx_b': nrm((L, RWKV_WIDTH), 0.01),
        'q_norm_g': 1.0 + nrm((L, Q_LORA), 0.02),
        'kv_norm_g': 1.0 + nrm((L, KV_LORA), 0.02),
        'w_uq': nrm((L, Q_LORA, MLA_HEADS * (MLA_NOPE + MLA_ROPE)), Q_LORA ** -0.5),
        'w_ukv': nrm((L, KV_LORA, MLA_HEADS * (MLA_NOPE + MLA_V_HEAD)), KV_LORA ** -0.5),
        'w_out': nrm((L, MIX_WIDTH, D), MIX_WIDTH ** -0.5),
        'final_g': 1.0 + nrm((D,), 0.02),
    }


def reference(x, c, positions, norm_g, w_ada, b_ada, w_in, w_vmix_down, mu_shift, mu_vmix,
              w0, w_decay_up, a0, w_iclr_up, v0, w_vmix_up, k_k, k_a, r_k, lnx_w, lnx_b,
              q_norm_g, kv_norm_g, w_uq, w_ukv, w_out, final_g):
    cos, sin = rope_tables(positions)
    c_act = jax.nn.silu(c)
    v_first = None
    for l in range(DEPTH):
        mod = c_act @ w_ada[l] + b_ada[l]
        shift, scale, gate = jnp.split(mod, 3, axis=-1)
        h = rms_norm(x, norm_g[l]) * (1 + scale[:, None, :]) + shift[:, None, :]
        if l == 0:
            proj = h @ w_in[0]
            p_vmix, mu_v, v0_l, w_vu = None, None, None, None
        else:
            proj = h @ jnp.concatenate([w_in[l], w_vmix_down[l - 1]], axis=1)
            p_vmix, mu_v, v0_l, w_vu = proj[..., IN_WIDTH:], mu_vmix[l - 1], v0[l - 1], w_vmix_up[l - 1]
        y_rwkv, v_first = rwkv7_time_mix(
            proj[..., :SHIFT_WIDTH], p_vmix, v_first, mu_shift[l], mu_v, w0[l], w_decay_up[l],
            a0[l], w_iclr_up[l], v0_l, w_vu, k_k[l], k_a[l], r_k[l], lnx_w[l], lnx_b[l])
        y_mla = mla_branch(proj[..., O_CQ:O_CKV], proj[..., O_CKV:O_KR], proj[..., O_KR:O_GM],
                           cos, sin, q_norm_g[l], kv_norm_g[l], w_uq[l], w_ukv[l])
        y = jnp.concatenate([y_rwkv * jax.nn.silu(proj[..., O_GR:O_CQ]),
                             y_mla * jax.nn.silu(proj[..., O_GM:IN_WIDTH])], axis=-1)
        x = x + gate[:, None, :] * (y @ w_out[l])
    return rms_norm(x, final_g)
```

```python
import functools

import jax
import jax.numpy as jnp
from jax import lax
from jax.experimental import pallas as pl
from jax.experimental.pallas import tpu as pltpu

F32 = jnp.float32
BF16 = jnp.bfloat16

D_MODEL = 1024
RWKV_WIDTH = 512
HEAD = 64
HEADS = 8
DECAY_LORA = 64
ICLR_LORA = 64
VMIX_LORA = 32
MLA_ROPE = 32
Q_LORA = 384
KV_LORA = 256
ROPE_THETA = 10000.0
NORM_EPS = 1e-6
GN_EPS = 64e-5
SHIFT_WIDTH = 3 * RWKV_WIDTH + DECAY_LORA + ICLR_LORA
IN_WIDTH = 3360

LANES = 128
MXU_DIM = 256
QUAD = MXU_DIM // HEAD
VMEM_LIMIT = 48 * 1024 * 1024

C_P = 0
C_Y = SHIFT_WIDTH
C_KV = C_Y + LANES
C_GR = C_KV + KV_LORA
C_GM = C_GR + RWKV_WIDTH
C_Q = C_GM + RWKV_WIDTH
P_WIDTH = C_Q + Q_LORA
KR_LANE = 64

CHUNK = 64

P_SCORE = 3
P_SOLVE = 3
P_OUT = 3
P_TRANS = 3
P_STATE = 3

NN = ((1,), (0,))
NT = ((1,), (1,))
TN = ((0,), (0,))


def _dot(a, b, dims=NN):
    return lax.dot_general(a, b, (dims, ((), ())), preferred_element_type=F32)


def _split(x):
    hi = x.astype(BF16)
    lo = (x - hi.astype(F32)).astype(BF16)
    return hi, lo


def _split3(x):
    h1 = x.astype(BF16)
    r1 = x - h1.astype(F32)
    h2 = r1.astype(BF16)
    h3 = (r1 - h2.astype(F32)).astype(BF16)
    return h1, h2, h3


def _mm(a, b, dims=NN, passes=3):
    out = _dot(a[0], b[0], dims)
    if passes >= 3:
        out = out + _dot(a[0], b[1], dims) + _dot(a[1], b[0], dims)
    return out


def _mm_exact_rhs(x, m):
    h1, h2, h3 = _split3(x)
    return _dot(h1, m) + _dot(h2, m) + _dot(h3, m)


def _rms(x, g):
    return x * lax.rsqrt(jnp.mean(x * x, axis=-1, keepdims=True) + NORM_EPS) * g


def _silu(x):
    return x * jax.nn.sigmoid(x)


def _mod_kernel(c_ref, w_ref, b_ref, o_ref):
    ca = _silu(c_ref[...])
    o_ref[0] = _mm(_split(ca), _split(w_ref[0]), NN, 3) + b_ref[0]


def _mod_call(c, w_ada, b_ada):
    L, D, D3 = w_ada.shape
    B = c.shape[0]
    nb = D3 // D
    return pl.pallas_call(
        _mod_kernel,
        grid=(L, nb),
        in_specs=[
            pl.BlockSpec((B, D), lambda l, j: (0, 0)),
            pl.BlockSpec((1, D, D), lambda l, j: (l, 0, j)),
            pl.BlockSpec((1, 1, D), lambda l, j: (l, 0, j)),
        ],
        out_specs=pl.BlockSpec((1, B, D), lambda l, j: (l, 0, j)),
        out_shape=jax.ShapeDtypeStruct((L, B, D3), F32),
        compiler_params=pltpu.CompilerParams(
            dimension_semantics=("arbitrary", "arbitrary"), vmem_limit_bytes=VMEM_LIMIT),
        name="adaln_mod",
    )(c, w_ada, b_ada.reshape(L, 1, D3))


def _rope_kernel(pos_ref, inv_ref, cosq_ref, cosk_ref, sina_ref, sinb_ref):
    ang = pos_ref[0].astype(F32) * inv_ref[...]
    cs = jnp.cos(ang)
    sn = jnp.sin(ang)
    lane = lax.broadcasted_iota(jnp.int32, ang.shape, 1)
    half = MLA_ROPE // 2
    in_rope = (lane >= KR_LANE) & (lane < KR_LANE + MLA_ROPE)
    first = (lane >= KR_LANE) & (lane < KR_LANE + half)
    second = (lane >= KR_LANE + half) & (lane < KR_LANE + MLA_ROPE)
    cosk = jnp.where(in_rope, cs, 0.0)
    cosk_ref[0] = cosk
    cosq_ref[0] = jnp.where(lane < KR_LANE, 1.0, cosk)
    sina_ref[0] = jnp.where(first, -sn, 0.0)
    sinb_ref[0] = jnp.where(second, sn, 0.0)


def _rope_call(positions):
    B, S = positions.shape
    ts = min(S, 512)
    half = MLA_ROPE // 2
    inv = ROPE_THETA ** (-jnp.arange(0, MLA_ROPE, 2, dtype=F32) / MLA_ROPE)
    inv_full = jnp.zeros((1, LANES), F32)
    inv_full = inv_full.at[0, KR_LANE:KR_LANE + half].set(inv)
    inv_full = inv_full.at[0, KR_LANE + half:KR_LANE + MLA_ROPE].set(inv)
    tab = jax.ShapeDtypeStruct((B, S, LANES), F32)
    spec = pl.BlockSpec((1, ts, LANES), lambda b, t: (b, t, 0))
    return pl.pallas_call(
        _rope_kernel,
        grid=(B, S // ts),
        in_specs=[pl.BlockSpec((1, ts, 1), lambda b, t: (b, t, 0)),
                  pl.BlockSpec((1, LANES), lambda b, t: (0, 0))],
        out_specs=[spec, spec, spec, spec],
        out_shape=[tab, tab, tab, tab],
        compiler_params=pltpu.CompilerParams(
            dimension_semantics=("arbitrary", "arbitrary"), vmem_limit_bytes=VMEM_LIMIT),
        name="rope_tables",
    )(positions.reshape(B, S, 1), inv_full)


def _inproj_kernel(x_ref, g_ref, sc_ref, sh_ref, w_ref, o_ref):
    h = _rms(x_ref[0], g_ref[...]) * (1.0 + sc_ref[0]) + sh_ref[0]
    o_ref[0] = _dot(h.astype(BF16), w_ref[...])


def _inproj_call(x, g, scale, shift, w):
    B, S, D = x.shape
    N = w.shape[1]
    tm = min(S, 256)
    return pl.pallas_call(
        _inproj_kernel,
        grid=(B, S // tm),
        in_specs=[
            pl.BlockSpec((1, tm, D), lambda b, t: (b, t, 0)),
            pl.BlockSpec((1, D), lambda b, t: (0, 0)),
            pl.BlockSpec((1, 1, D), lambda b, t: (b, 0, 0)),
            pl.BlockSpec((1, 1, D), lambda b, t: (b, 0, 0)),
            pl.BlockSpec((D, N), lambda b, t: (0, 0)),
        ],
        out_specs=pl.BlockSpec((1, tm, N), lambda b, t: (b, t, 0)),
        out_shape=jax.ShapeDtypeStruct((B, S, N), F32),
        compiler_params=pltpu.CompilerParams(
            dimension_semantics=("arbitrary", "arbitrary"), vmem_limit_bytes=VMEM_LIMIT),
        name="inproj",
    )(x, g.reshape(1, D), scale.reshape(B, 1, D), shift.reshape(B, 1, D), w)


def _expand(xs, bd):
    return tuple(jnp.concatenate([x] * QUAD, axis=0) * bd for x in xs)


def _rwkv_kernel(*refs, has_vmix, ts):
    it = iter(refs)
    p_ref = next(it)
    gr_ref = next(it)
    if has_vmix:
        yb_ref = next(it)
        vf_ref = next(it)
    mu_ref = next(it)
    vecs_ref = next(it)
    wdec_ref = next(it)
    wicl_ref = next(it)
    if has_vmix:
        muy_ref = next(it)
        wvm_ref = next(it)
    ones_ref = next(it)
    tri_ref = next(it)
    bd_ref = next(it)
    out_ref = next(it)
    if not has_vmix:
        vf_out_ref = next(it)
    carry_p = next(it)
    carry_y = next(it)
    state = next(it)
    r_s, k_s, v_s, kk_s, b_s, lw_s, y_s = (next(it) for _ in range(7))

    t = pl.program_id(1)

    @pl.when(t == 0)
    def _():
        carry_p[...] = jnp.zeros_like(carry_p)
        carry_y[...] = jnp.zeros_like(carry_y)
        state[...] = jnp.zeros_like(state)

    w0 = vecs_ref[0:1, :]
    a0 = vecs_ref[1:2, :]
    k_k = vecs_ref[2:3, :]
    k_a = vecs_ref[3:4, :]
    r_k = vecs_ref[4:5, :]
    lnw = vecs_ref[5:6, :]
    lnb = vecs_ref[6:7, :]
    ones = ones_ref[...]

    def seg(x):
        return _mm_exact_rhs(x, ones)

    p = p_ref[0]
    row = lax.broadcasted_iota(jnp.int32, (ts, 1), 0)
    prev = jnp.where(row == 0, carry_p[...], pltpu.roll(p, 1, 0))
    carry_p[...] = p[ts - 1:ts, :]
    ps = p + (prev - p) * mu_ref[...]
    r = ps[:, 0:RWKV_WIDTH]
    k = ps[:, RWKV_WIDTH:2 * RWKV_WIDTH]
    v = ps[:, 2 * RWKV_WIDTH:3 * RWKV_WIDTH]
    xl = ps[:, 3 * RWKV_WIDTH:SHIFT_WIDTH]

    dec = w0 + _mm(_split(jnp.tanh(xl)), (wdec_ref[0], wdec_ref[1]))
    z = -dec
    w_log = -(jnp.maximum(z, 0.0) + jnp.log1p(jnp.exp(-jnp.abs(z)))) - 0.5
    lw = -jnp.exp(w_log)
    a = jax.nn.sigmoid(a0 + _mm(_split(xl), (wicl_ref[0], wicl_ref[1])))

    if has_vmix:
        yb = yb_ref[0]
        prevy = jnp.where(row == 0, carry_y[...], pltpu.roll(yb, 1, 0))
        carry_y[...] = yb[ts - 1:ts, :]
        ys = yb + (prevy - yb) * muy_ref[...]
        v0 = vecs_ref[7:8, :]
        mix = jax.nn.sigmoid(v0 + _mm(_split(ys), (wvm_ref[0], wvm_ref[1])))
        v = v + (vf_ref[0] - v) * mix
    else:
        vf_out_ref[0] = v

    kk = k * k_k
    kk = kk / jnp.maximum(jnp.sqrt(seg(kk * kk)), 1e-12)
    k2 = k * (1.0 + (a - 1.0) * k_a)
    bonus = seg(r * k2 * r_k) * v

    r_s[...] = r
    k_s[...] = k2
    v_s[...] = v
    kk_s[...] = kk
    b_s[...] = kk * a
    lw_s[...] = lw

    tri = tri_ref[...]
    bd = bd_ref[...]
    ri = lax.broadcasted_iota(jnp.int32, (CHUNK, MXU_DIM), 0)
    ci = lax.broadcasted_iota(jnp.int32, (CHUNK, MXU_DIM), 1) & (HEAD - 1)
    strict = ci < ri
    incl = ci <= ri
    eye_sb = jnp.where(ci == ri, 1.0, 0.0).astype(F32)
    er = lax.broadcasted_iota(jnp.int32, (MXU_DIM, MXU_DIM), 0)
    ec = lax.broadcasted_iota(jnp.int32, (MXU_DIM, MXU_DIM), 1)
    eye_bd = er == ec
    bdm = (er // HEAD) == (ec // HEAD)

    def chunk_body(c, carry):
        rows = pl.ds(pl.multiple_of(c * CHUNK, CHUNK), CHUNK)
        lwc = lw_s[rows, :]
        cl = _mm_exact_rhs_left(tri, lwc)
        clc = cl[CHUNK - 1:CHUNK, :]
        e_in = jnp.exp(cl)
        e_out = jnp.exp(-cl)
        e_prev = jnp.exp(cl - lwc)
        pc = jnp.exp(clc)
        rh = r_s[rows, :] * e_in
        kh = k_s[rows, :] * e_out
        ah = kk_s[rows, :] * e_prev
        bh = b_s[rows, :] * e_out
        kp = kh * pc
        bp = bh * pc
        vv = v_s[rows, :]
        for q in range(RWKV_WIDTH // MXU_DIM):
            sl = slice(q * MXU_DIM, (q + 1) * MXU_DIM)
            R, Kh, A, Bh, Kp, Bp, V = (x[:, sl] for x in (rh, kh, ah, bh, kp, bp, vv))
            As, Rs, Vs = _split(A), _split(R), _split(V)
            Bx = _expand(_split(Bh), bd)
            Kx = _expand(_split(Kh), bd)
            Vx = _expand(Vs, bd)
            Ax = _expand(As, bd)
            lab = jnp.where(strict, _mm(As, Bx, NT, P_SCORE), 0.0)
            lak = jnp.where(strict, _mm(As, Kx, NT, P_SCORE), 0.0)
            mrb = jnp.where(incl, _mm(Rs, Bx, NT, P_SCORE), 0.0)
            mrk = jnp.where(incl, _mm(Rs, Kx, NT, P_SCORE), 0.0)
            T = eye_sb - lab
            P = lab
            n = 2
            while n < CHUNK:
                pb = P.astype(BF16)
                P = _dot(pb, _expand((pb,), bd)[0])
                T = T + _dot(T.astype(BF16), _expand((P.astype(BF16),), bd)[0])
                n *= 2
            Ts = _split(T)
            lakv = _mm(_split(lak), Vx, NN, P_SOLVE)
            W = _mm(Ts, Ax, NN, P_SOLVE)
            U0 = _mm(Ts, _expand(_split(lakv), bd), NN, P_SOLVE)
            Ws, U0s = _split(W), _split(U0)
            Wx = _expand(Ws, bd)
            U0x = _expand(U0s, bd)
            mrbs = _split(mrb)
            Qt = R - _mm(mrbs, Wx, NN, P_OUT)
            Y0 = _mm(_split(mrk), Vx, NN, P_OUT) - _mm(mrbs, U0x, NN, P_OUT)
            Bps = _split(Bp)
            G = (jnp.where(eye_bd, jnp.broadcast_to(pc[:, sl], (MXU_DIM, MXU_DIM)), 0.0)
                 - jnp.where(bdm, _mm(Ws, Bps, TN, P_TRANS), 0.0))
            Hbd = jnp.where(bdm, _mm(Vs, _split(Kp), TN, P_TRANS) - _mm(U0s, Bps, TN, P_TRANS), 0.0)
            Hm = Hbd[0:HEAD] + Hbd[HEAD:2 * HEAD] + Hbd[2 * HEAD:3 * HEAD] + Hbd[3 * HEAD:4 * HEAD]
            S = state[q]
            Ss = _split(S)
            y_s[rows, sl] = _mm(_split(Qt), _expand(Ss, bd), NT, P_STATE) + Y0
            state[q] = _mm(Ss, _split(G), NN, P_STATE) + Hm
        return carry

    lax.fori_loop(0, ts // CHUNK, chunk_body, 0)

    y = y_s[...]
    mean = seg(y) * (1.0 / HEAD)
    d = y - mean
    var = seg(d * d) * (1.0 / HEAD)
    yn = d * lax.rsqrt(var + GN_EPS) * lnw + lnb
    out_ref[0] = ((yn + bonus) * _silu(gr_ref[0])).astype(BF16)


def _mm_exact_rhs_left(m, x):
    h1, h2, h3 = _split3(x)
    return _dot(m, h1) + _dot(m, h2) + _dot(m, h3)


def _rwkv_call(proj, v_first, lw):
    B, S, _ = proj.shape
    has_vmix = v_first is not None
    ts = min(S, 256)
    nP = SHIFT_WIDTH
    tok = lambda w, j: pl.BlockSpec((1, ts, w), lambda b, t: (b, t, j))
    full = lambda a: pl.BlockSpec(a.shape, lambda b, t: (0,) * a.ndim)

    args = [proj, proj]
    specs = [tok(nP, C_P // nP), tok(RWKV_WIDTH, C_GR // RWKV_WIDTH)]
    if has_vmix:
        args += [proj, v_first]
        specs += [tok(LANES, C_Y // LANES), tok(RWKV_WIDTH, 0)]
    consts = [lw["mu"], lw["vecs"], lw["wdec"], lw["wicl"]]
    if has_vmix:
        consts += [lw["muy"], lw["wvm"]]
    consts += [lw["ones"], lw["tri"], lw["bd"]]
    args += consts
    specs += [full(a) for a in consts]

    out_shape = [jax.ShapeDtypeStruct((B, S, RWKV_WIDTH), BF16)]
    out_specs = [tok(RWKV_WIDTH, 0)]
    if not has_vmix:
        out_shape.append(jax.ShapeDtypeStruct((B, S, RWKV_WIDTH), F32))
        out_specs.append(tok(RWKV_WIDTH, 0))

    big = pltpu.VMEM((ts, RWKV_WIDTH), F32)
    scratch = [pltpu.VMEM((1, nP), F32), pltpu.VMEM((1, LANES), F32),
               pltpu.VMEM((RWKV_WIDTH // MXU_DIM, HEAD, MXU_DIM), F32)] + [big] * 7
    outs = pl.pallas_call(
        functools.partial(_rwkv_kernel, has_vmix=has_vmix, ts=ts),
        grid=(B, S // ts),
        in_specs=specs,
        out_specs=out_specs,
        out_shape=out_shape,
        scratch_shapes=scratch,
        compiler_params=pltpu.CompilerParams(
            dimension_semantics=("arbitrary", "arbitrary"), vmem_limit_bytes=VMEM_LIMIT),
        name="rwkv_vmix" if has_vmix else "rwkv_first",
    )(*args)
    if has_vmix:
        return outs[0], v_first
    return outs[0], outs[1]


def _rot_half(x, sina, sinb):
    half = MLA_ROPE // 2
    return pltpu.roll(x, LANES - half, 1) * sina + pltpu.roll(x, half, 1) * sinb


def _mla_prep_kernel(cq_ref, ckv_ref, yb_ref, cosq_ref, cosk_ref, sina_ref, sinb_ref,
                     gq_ref, gkv_ref, wq_ref, wkv_ref, q_ref, k_ref, v_ref, *, scale):
    cosq, cosk, sina, sinb = cosq_ref[0], cosk_ref[0], sina_ref[0], sinb_ref[0]
    cqn = _rms(cq_ref[0], gq_ref[...]).astype(BF16)
    qall = _dot(cqn, wq_ref[...])
    ckvn = _rms(ckv_ref[0], gkv_ref[...]).astype(BF16)
    kvall = _dot(ckvn, wkv_ref[...])
    yb = yb_ref[0]
    kr = yb * cosk + _rot_half(yb, sina, sinb)
    for h in range(HEADS):
        qh = qall[:, h * LANES:(h + 1) * LANES]
        q_ref[0, h] = ((qh * cosq + _rot_half(qh, sina, sinb)) * scale).astype(BF16)
        k_ref[0, h] = (kvall[:, h * LANES:(h + 1) * LANES] + kr).astype(BF16)
        v_ref[0, h] = kvall[:, (HEADS + h) * LANES:(HEADS + h + 1) * LANES].astype(BF16)


def _mla_prep_call(proj, tabs, gq, gkv, wq, wkv):
    B, S, _ = proj.shape
    tm = min(S, 256)
    tok = lambda w, j: pl.BlockSpec((1, tm, w), lambda b, t: (b, t, j))
    full = lambda a: pl.BlockSpec(a.shape, lambda b, t: (0,) * a.ndim)
    slab = jax.ShapeDtypeStruct((B, HEADS, S, LANES), BF16)
    slab_spec = pl.BlockSpec((1, HEADS, tm, LANES), lambda b, t: (b, 0, t, 0))
    scale = float(HEAD + MLA_ROPE) ** -0.5
    return pl.pallas_call(
        functools.partial(_mla_prep_kernel, scale=scale),
        grid=(B, S // tm),
        in_specs=[tok(Q_LORA, C_Q // Q_LORA), tok(KV_LORA, C_KV // KV_LORA), tok(LANES, C_Y // LANES),
                  tok(LANES, 0), tok(LANES, 0), tok(LANES, 0), tok(LANES, 0),
                  full(gq), full(gkv), full(wq), full(wkv)],
        out_specs=[slab_spec, slab_spec, slab_spec],
        out_shape=[slab, slab, slab],
        compiler_params=pltpu.CompilerParams(
            dimension_semantics=("arbitrary", "arbitrary"), vmem_limit_bytes=VMEM_LIMIT),
        name="mla_prep",
    )(proj, proj, proj, *tabs, gq, gkv, wq, wkv)


def _attn_kernel(q_ref, k_ref, v_ref, g_ref, o_ref, m_s, l_s, acc_s, *, blk):
    qi = pl.program_id(2)
    ki = pl.program_id(3)

    @pl.when(ki == 0)
    def _():
        m_s[...] = jnp.full_like(m_s, -jnp.inf)
        l_s[...] = jnp.zeros_like(l_s)
        acc_s[...] = jnp.zeros_like(acc_s)

    @pl.when(ki <= qi)
    def _():
        rr = lax.broadcasted_iota(jnp.int32, (blk, blk), 0)
        cc = lax.broadcasted_iota(jnp.int32, (blk, blk), 1)
        dead = (cc > rr) & (ki == qi)
        for hh in range(2):
            s = _dot(q_ref[0, hh], k_ref[0, hh], NT)
            s = jnp.where(dead, -jnp.inf, s)
            m_prev = m_s[hh]
            m_new = jnp.maximum(m_prev, jnp.max(s, axis=1, keepdims=True))
            p = jnp.exp(s - m_new[:, 0:1])
            alpha = jnp.exp(m_prev - m_new)
            l_s[hh] = alpha * l_s[hh] + jnp.sum(p, axis=1, keepdims=True)
            acc_s[hh] = alpha * acc_s[hh] + _dot(p.astype(BF16), v_ref[0, hh])
            m_s[hh] = m_new

    @pl.when(ki == qi)
    def _():
        o = acc_s[0] / l_s[0] + acc_s[1] / l_s[1]
        o_ref[0] = (o * _silu(g_ref[0])).astype(BF16)


def _attn_call(q, k, v, proj):
    B, H, S, _ = q.shape
    blk = min(S, 512)
    n = S // blk
    gm0 = C_GM // LANES
    qspec = pl.BlockSpec((1, 2, blk, LANES), lambda b, p, i, j: (b, p, i, 0))
    kspec = pl.BlockSpec((1, 2, blk, LANES), lambda b, p, i, j: (b, p, jnp.minimum(i, j), 0))
    return pl.pallas_call(
        functools.partial(_attn_kernel, blk=blk),
        grid=(B, H // 2, n, n),
        in_specs=[qspec, kspec, kspec,
                  pl.BlockSpec((1, blk, LANES), lambda b, p, i, j: (b, i, gm0 + p))],
        out_specs=pl.BlockSpec((1, blk, LANES), lambda b, p, i, j: (b, i, p)),
        out_shape=jax.ShapeDtypeStruct((B, S, H * HEAD), BF16),
        scratch_shapes=[pltpu.VMEM((2, blk, LANES), F32)] * 3,
        compiler_params=pltpu.CompilerParams(
            dimension_semantics=("arbitrary",) * 4, vmem_limit_bytes=VMEM_LIMIT),
        name="mla_attn",
    )(q, k, v, proj)


def _outproj_kernel(yr_ref, ym_ref, x_ref, gate_ref, w_ref, fg_ref, o_ref, *, final):
    y = _dot(yr_ref[0], w_ref[0:RWKV_WIDTH, :]) + _dot(ym_ref[0], w_ref[RWKV_WIDTH:, :])
    xn = x_ref[0] + gate_ref[0] * y
    if final:
        xn = _rms(xn, fg_ref[...])
    o_ref[0] = xn


def _outproj_call(yr, ym, x, gate, w, fg, final):
    B, S, D = x.shape
    tm = min(S, 512)
    tok = lambda w_: pl.BlockSpec((1, tm, w_), lambda b, t: (b, t, 0))
    return pl.pallas_call(
        functools.partial(_outproj_kernel, final=final),
        grid=(B, S // tm),
        in_specs=[tok(RWKV_WIDTH), tok(RWKV_WIDTH), tok(D),
                  pl.BlockSpec((1, 1, D), lambda b, t: (b, 0, 0)),
                  pl.BlockSpec(w.shape, lambda b, t: (0, 0)),
                  pl.BlockSpec((1, D), lambda b, t: (0, 0))],
        out_specs=tok(D),
        out_shape=jax.ShapeDtypeStruct((B, S, D), F32),
        compiler_params=pltpu.CompilerParams(
            dimension_semantics=("arbitrary", "arbitrary"), vmem_limit_bytes=VMEM_LIMIT),
        name="outproj_final" if final else "outproj",
    )(yr, ym, x, gate.reshape(B, 1, D), w, fg.reshape(1, D))


def _hilo(w):
    hi = w.astype(BF16)
    lo = (w - hi.astype(F32)).astype(BF16)
    return jnp.stack([hi, lo])


def _pack_w_in(w_in_l, w_vmix_down_l):
    D = w_in_l.shape[0]
    z = lambda n: jnp.zeros((D, n), F32)
    o_kr = SHIFT_WIDTH + RWKV_WIDTH + Q_LORA + KV_LORA
    o_cq = SHIFT_WIDTH + RWKV_WIDTH
    o_ckv = o_cq + Q_LORA
    o_gm = o_kr + MLA_ROPE
    vm = z(VMIX_LORA) if w_vmix_down_l is None else w_vmix_down_l
    yblk = jnp.concatenate([vm, z(KR_LANE - VMIX_LORA), w_in_l[:, o_kr:o_gm],
                            z(LANES - KR_LANE - MLA_ROPE)], axis=1)
    cols = [w_in_l[:, :SHIFT_WIDTH], yblk, w_in_l[:, o_ckv:o_kr],
            w_in_l[:, SHIFT_WIDTH:o_cq], w_in_l[:, o_gm:IN_WIDTH], w_in_l[:, o_cq:o_ckv]]
    return jnp.concatenate(cols, axis=1).astype(BF16)


def _pack_wq(w_uq_l):
    w = w_uq_l.reshape(Q_LORA, HEADS, HEAD + MLA_ROPE)
    w = jnp.pad(w, ((0, 0), (0, 0), (0, LANES - HEAD - MLA_ROPE)))
    return w.reshape(Q_LORA, HEADS * LANES).astype(BF16)


def _pack_wkv(w_ukv_l):
    w = w_ukv_l.reshape(KV_LORA, HEADS, 2 * HEAD)
    wk = jnp.pad(w[:, :, :HEAD], ((0, 0), (0, 0), (0, LANES - HEAD)))
    wv = w[:, :, HEAD:]
    zero = jnp.zeros_like(wv)
    even = jnp.concatenate([wv, zero], axis=-1)
    odd = jnp.concatenate([zero, wv], axis=-1)
    is_odd = (jnp.arange(HEADS) % 2 == 1)[None, :, None]
    wv2 = jnp.where(is_odd, odd, even)
    return jnp.concatenate([wk.reshape(KV_LORA, -1), wv2.reshape(KV_LORA, -1)], axis=1).astype(BF16)


def _pad_rows(w, lo, total):
    return jnp.pad(w, ((lo, total - lo - w.shape[0]), (0, 0)))


def kernel(x, c, positions, norm_g, w_ada, b_ada, w_in, w_vmix_down, mu_shift, mu_vmix, w0, w_decay_up, a0, w_iclr_up, v0, w_vmix_up, k_k, k_a, r_k, lnx_w, lnx_b, q_norm_g, kv_norm_g, w_uq, w_ukv, w_out, final_g):
    B, S, D = x.shape
    L = w_in.shape[0]
    assert D == D_MODEL and S % CHUNK == 0

    mod = _mod_call(c, w_ada, b_ada)
    tabs = _rope_call(positions)

    hid = jnp.arange(RWKV_WIDTH) // HEAD
    ones = (hid[:, None] == hid[None, :]).astype(BF16)
    tri = (jnp.arange(CHUNK)[:, None] >= jnp.arange(CHUNK)[None, :]).astype(BF16)
    qid = jnp.arange(MXU_DIM) // HEAD
    bd = (qid[:, None] == qid[None, :]).astype(BF16)

    v_first = None
    for l in range(L):
        shift, scale, gate = mod[l, :, :D], mod[l, :, D:2 * D], mod[l, :, 2 * D:]
        w_in_p = _pack_w_in(w_in[l], None if l == 0 else w_vmix_down[l - 1])
        proj = _inproj_call(x, norm_g[l], scale, shift, w_in_p)

        vecs = jnp.stack([w0[l], a0[l], k_k[l], k_a[l], r_k[l].reshape(-1), lnx_w[l], lnx_b[l],
                          v0[l - 1] if l > 0 else jnp.zeros((RWKV_WIDTH,), F32)])
        lw = {
            "mu": mu_shift[l].reshape(1, SHIFT_WIDTH),
            "vecs": vecs,
            "wdec": _hilo(_pad_rows(w_decay_up[l], 0, LANES)),
            "wicl": _hilo(_pad_rows(w_iclr_up[l], DECAY_LORA, LANES)),
            "ones": ones, "tri": tri, "bd": bd,
        }
        if l > 0:
            lw["muy"] = jnp.pad(mu_vmix[l - 1], (0, LANES - VMIX_LORA)).reshape(1, LANES)
            lw["wvm"] = _hilo(_pad_rows(w_vmix_up[l - 1], 0, LANES))
        y_rwkv, v_first = _rwkv_call(proj, v_first, lw)

        q, k, v = _mla_prep_call(proj, tabs, q_norm_g[l].reshape(1, Q_LORA),
                                 kv_norm_g[l].reshape(1, KV_LORA), _pack_wq(w_uq[l]), _pack_wkv(w_ukv[l]))
        y_mla = _attn_call(q, k, v, proj)

        x = _outproj_call(y_rwkv, y_mla, x, gate, w_out[l].astype(BF16), final_g, final=(l == L - 1))
    return x
```

```python
import functools

import jax
import jax.numpy as jnp
from jax import lax
from jax.experimental import pallas as pl
from jax.experimental.pallas import tpu as pltpu

F32 = jnp.float32
BF16 = jnp.bfloat16

D_MODEL = 1024
RWKV_WIDTH = 512
HEAD = 64
HEADS = 8
DECAY_LORA = 64
ICLR_LORA = 64
VMIX_LORA = 32
MLA_ROPE = 32
Q_LORA = 384
KV_LORA = 256
ROPE_THETA = 10000.0
NORM_EPS = 1e-6
GN_EPS = 64e-5
SHIFT_WIDTH = 3 * RWKV_WIDTH + DECAY_LORA + ICLR_LORA
IN_WIDTH = 3360

LANES = 128
MXU_DIM = 256
QUAD = MXU_DIM // HEAD
VMEM_LIMIT = 48 * 1024 * 1024

C_P = 0
C_Y = SHIFT_WIDTH
C_KV = C_Y + LANES
C_GR = C_KV + KV_LORA
C_GM = C_GR + RWKV_WIDTH
C_Q = C_GM + RWKV_WIDTH
P_WIDTH = C_Q + Q_LORA
KR_LANE = 64

CHUNK = 64

NN = ((1,), (0,))
NT = ((1,), (1,))
TN = ((0,), (0,))


def _dot(a, b, dims=NN):
    return lax.dot_general(a, b, (dims, ((), ())), preferred_element_type=F32)


def _split(x):
    hi = x.astype(BF16)
    lo = (x - hi.astype(F32)).astype(BF16)
    return hi, lo


def _split3(x):
    h1 = x.astype(BF16)
    r1 = x - h1.astype(F32)
    h2 = r1.astype(BF16)
    h3 = (r1 - h2.astype(F32)).astype(BF16)
    return h1, h2, h3


def _mm(a, b, dims=NN, passes=3):
    out = _dot(a[0], b[0], dims)
    if passes >= 3:
        out = out + _dot(a[0], b[1], dims) + _dot(a[1], b[0], dims)
    return out


def _mm_exact_rhs(x, m):
    h1, h2, h3 = _split3(x)
    return _dot(h1, m) + _dot(h2, m) + _dot(h3, m)


def _rms(x, g):
    return x * lax.rsqrt(jnp.mean(x * x, axis=-1, keepdims=True) + NORM_EPS) * g


def _silu(x):
    return x * jax.nn.sigmoid(x)


def _mod_kernel(c_ref, w_ref, b_ref, o_ref):
    ca = _silu(c_ref[...])
    o_ref[0] = _mm(_split(ca), _split(w_ref[0]), NN, 3) + b_ref[0]


def _mod_call(c, w_ada, b_ada):
    L, D, D3 = w_ada.shape
    B = c.shape[0]
    nb = D3 // D
    return pl.pallas_call(
        _mod_kernel,
        grid=(L, nb),
        in_specs=[
            pl.BlockSpec((B, D), lambda l, j: (0, 0)),
            pl.BlockSpec((1, D, D), lambda l, j: (l, 0, j)),
            pl.BlockSpec((1, 1, D), lambda l, j: (l, 0, j)),
        ],
        out_specs=pl.BlockSpec((1, B, D), lambda l, j: (l, 0, j)),
        out_shape=jax.ShapeDtypeStruct((L, B, D3), F32),
        compiler_params=pltpu.CompilerParams(
            dimension_semantics=("arbitrary", "arbitrary"), vmem_limit_bytes=VMEM_LIMIT),
        name="adaln_mod",
    )(c, w_ada, b_ada.reshape(L, 1, D3))


def _rope_kernel(pos_ref, inv_ref, cosq_ref, cosk_ref, sina_ref, sinb_ref):
    ang = pos_ref[0].astype(F32) * inv_ref[...]
    cs = jnp.cos(ang)
    sn = jnp.sin(ang)
    lane = lax.broadcasted_iota(jnp.int32, ang.shape, 1)
    half = MLA_ROPE // 2
    in_rope = (lane >= KR_LANE) & (lane < KR_LANE + MLA_ROPE)
    first = (lane >= KR_LANE) & (lane < KR_LANE + half)
    second = (lane >= KR_LANE + half) & (lane < KR_LANE + MLA_ROPE)
    cosk = jnp.where(in_rope, cs, 0.0)
    cosk_ref[0] = cosk
    cosq_ref[0] = jnp.where(lane < KR_LANE, 1.0, cosk)
    sina_ref[0] = jnp.where(first, -sn, 0.0)
    sinb_ref[0] = jnp.where(second, sn, 0.0)


def _rope_call(positions):
    B, S = positions.shape
    ts = min(S, 512)
    half = MLA_ROPE // 2
    inv = ROPE_THETA ** (-jnp.arange(0, MLA_ROPE, 2, dtype=F32) / MLA_ROPE)
    inv_full = jnp.zeros((1, LANES), F32)
    inv_full = inv_full.at[0, KR_LANE:KR_LANE + half].set(inv)
    inv_full = inv_full.at[0, KR_LANE + half:KR_LANE + MLA_ROPE].set(inv)
    tab = jax.ShapeDtypeStruct((B, S, LANES), F32)
    spec = pl.BlockSpec((1, ts, LANES), lambda b, t: (b, t, 0))
    return pl.pallas_call(
        _rope_kernel,
        grid=(B, S // ts),
        in_specs=[pl.BlockSpec((1, ts, 1), lambda b, t: (b, t, 0)),
                  pl.BlockSpec((1, LANES), lambda b, t: (0, 0))],
        out_specs=[spec, spec, spec, spec],
        out_shape=[tab, tab, tab, tab],
        compiler_params=pltpu.CompilerParams(
            dimension_semantics=("arbitrary", "arbitrary"), vmem_limit_bytes=VMEM_LIMIT),
        name="rope_tables",
    )(positions.reshape(B, S, 1), inv_full)


def _inproj_kernel(x_ref, g_ref, sc_ref, sh_ref, w_ref, o_ref):
    h = _rms(x_ref[0], g_ref[...]) * (1.0 + sc_ref[0]) + sh_ref[0]
    o_ref[0] = _dot(h.astype(BF16), w_ref[...])


def _inproj_call(x, g, scale, shift, w):
    B, S, D = x.shape
    N = w.shape[1]
    tm = min(S, 256)
    return pl.pallas_call(
        _inproj_kernel,
        grid=(B, S // tm),
        in_specs=[
            pl.BlockSpec((1, tm, D), lambda b, t: (b, t, 0)),
            pl.BlockSpec((1, D), lambda b, t: (0, 0)),
            pl.BlockSpec((1, 1, D), lambda b, t: (b, 0, 0)),
            pl.BlockSpec((1, 1, D), lambda b, t: (b, 0, 0)),
            pl.BlockSpec((D, N), lambda b, t: (0, 0)),
        ],
        out_specs=pl.BlockSpec((1, tm, N), lambda b, t: (b, t, 0)),
        out_shape=jax.ShapeDtypeStruct((B, S, N), F32),
        compiler_params=pltpu.CompilerParams(
            dimension_semantics=("arbitrary", "arbitrary"), vmem_limit_bytes=VMEM_LIMIT),
        name="inproj",
    )(x, g.reshape(1, D), scale.reshape(B, 1, D), shift.reshape(B, 1, D), w)


def _rwkv_kernel(*refs, has_vmix, ts):
    it = iter(refs)
    p_ref = next(it)
    gr_ref = next(it)
    if has_vmix:
        yb_ref = next(it)
        vf_ref = next(it)
    mu_ref = next(it)
    vecs_ref = next(it)
    wdec_ref = next(it)
    wicl_ref = next(it)
    if has_vmix:
        muy_ref = next(it)
        wvm_ref = next(it)
    ones_ref = next(it)
    tri_ref = next(it)
    tot_ref = next(it)
    bd_ref = next(it)
    out_ref = next(it)
    if not has_vmix:
        vf_out_ref = next(it)
    carry_p = next(it)
    carry_y = next(it)
    state = next(it)
    rh_s, pc_s, y_s, ops_s, g_s, h_s, qt_s = (next(it) for _ in range(7))

    t = pl.program_id(1)

    @pl.when(t == 0)
    def _():
        carry_p[...] = jnp.zeros_like(carry_p)
        carry_y[...] = jnp.zeros_like(carry_y)
        state[...] = jnp.zeros_like(state)

    w0 = vecs_ref[0:1, :]
    a0 = vecs_ref[1:2, :]
    k_k = vecs_ref[2:3, :]
    k_a = vecs_ref[3:4, :]
    r_k = vecs_ref[4:5, :]
    lnw = vecs_ref[5:6, :]
    lnb = vecs_ref[6:7, :]
    ones = ones_ref[...]

    def seg(x):
        return _mm_exact_rhs(x, ones)

    p = p_ref[0]
    row = lax.broadcasted_iota(jnp.int32, (ts, 1), 0)
    prev = jnp.where(row == 0, carry_p[...], pltpu.roll(p, 1, 0))
    carry_p[...] = p[ts - 1:ts, :]
    ps = p + (prev - p) * mu_ref[...]
    r = ps[:, 0:RWKV_WIDTH]
    k = ps[:, RWKV_WIDTH:2 * RWKV_WIDTH]
    v = ps[:, 2 * RWKV_WIDTH:3 * RWKV_WIDTH]
    xl = ps[:, 3 * RWKV_WIDTH:SHIFT_WIDTH]

    dec = w0 + _mm(_split(jnp.tanh(xl)), (wdec_ref[0], wdec_ref[1]))
    z = -dec
    w_log = -(jnp.maximum(z, 0.0) + jnp.log1p(jnp.exp(-jnp.abs(z)))) - 0.5
    lw = -jnp.exp(w_log)
    a = jax.nn.sigmoid(a0 + _mm(_split(xl), (wicl_ref[0], wicl_ref[1])))

    if has_vmix:
        yb = yb_ref[0]
        prevy = jnp.where(row == 0, carry_y[...], pltpu.roll(yb, 1, 0))
        carry_y[...] = yb[ts - 1:ts, :]
        ys = yb + (prevy - yb) * muy_ref[...]
        v0 = vecs_ref[7:8, :]
        mix = jax.nn.sigmoid(v0 + _mm(_split(ys), (wvm_ref[0], wvm_ref[1])))
        v = v + (vf_ref[0] - v) * mix
    else:
        vf_out_ref[0] = v

    kk = k * k_k
    kk = kk / jnp.maximum(jnp.sqrt(seg(kk * kk)), 1e-12)
    k2 = k * (1.0 + (a - 1.0) * k_a)
    bonus = seg(r * k2 * r_k) * v

    cl = _mm_exact_rhs_left(tri_ref[...], lw)
    ce = _mm_exact_rhs_left(tot_ref[...], lw)
    e_out = jnp.exp(-cl)
    pce = jnp.exp(ce)
    rh = r * jnp.exp(cl)
    kh = k2 * e_out
    bh = kk * a * e_out
    bf = lambda x_: x_.astype(BF16)
    rh_s[...] = rh
    pc_s[...] = pce
    ops_s[0] = bf(kk * jnp.exp(cl - lw))
    ops_s[1] = bf(rh)
    ops_s[2] = bf(bh)
    ops_s[3] = bf(kh)
    ops_s[4] = bf(v)
    ops_s[5] = bf(kh * pce)
    ops_s[6] = bf(bh * pce)

    bd = bd_ref[...]
    ri = lax.broadcasted_iota(jnp.int32, (CHUNK, MXU_DIM), 0)
    ci = lax.broadcasted_iota(jnp.int32, (CHUNK, MXU_DIM), 1) & (HEAD - 1)
    strict = ci < ri
    incl = ci <= ri
    eye_sb = jnp.where(ci == ri, 1.0, 0.0).astype(F32)
    er = lax.broadcasted_iota(jnp.int32, (MXU_DIM, MXU_DIM), 0)
    ec = lax.broadcasted_iota(jnp.int32, (MXU_DIM, MXU_DIM), 1)
    eye_bd = er == ec
    bdm = (er // HEAD) == (ec // HEAD)

    def expand(xb):
        return jnp.concatenate([xb] * QUAD, axis=0) * bd

    nchunk = ts // CHUNK
    nquad = RWKV_WIDTH // MXU_DIM
    chains = [(c, q) for c in range(nchunk) for q in range(nquad)]
    rows = lambda c: slice(c * CHUNK, (c + 1) * CHUNK)
    lanes = lambda q: slice(q * MXU_DIM, (q + 1) * MXU_DIM)
    op = lambda j: [ops_s[j, rows(c), lanes(q)] for c, q in chains]
    each = lambda f, *ls: [f(*xs) for xs in zip(*ls)]

    Ab, Rb, Bhb, Khb, Vb, Kpb, Bpb = (op(j) for j in range(7))
    AR = each(lambda a_, r_: jnp.concatenate([a_, r_], axis=0), Ab, Rb)
    sb = each(lambda x_, b_: _dot(x_, expand(b_), NT), AR, Bhb)
    sk = each(lambda x_, k_: _dot(x_, expand(k_), NT), AR, Khb)
    lab = [jnp.where(strict, x_[:CHUNK], 0.0) for x_ in sb]
    mrb = [bf(jnp.where(incl, x_[CHUNK:], 0.0)) for x_ in sb]
    lak = [bf(jnp.where(strict, x_[:CHUNK], 0.0)) for x_ in sk]
    mrk = [bf(jnp.where(incl, x_[CHUNK:], 0.0)) for x_ in sk]
    T = [eye_sb - l_ for l_ in lab]
    P = lab
    Px = [expand(bf(p_)) for p_ in P]
    n = 2
    while n < CHUNK:
        P = each(lambda p_, x_: _dot(bf(p_), x_), P, Px)
        Px = [expand(bf(p_)) for p_ in P]
        T = each(lambda t_, x_: t_ + _dot(bf(t_), x_), T, Px)
        n *= 2
    Tb = [bf(t_) for t_ in T]
    Vx = [expand(v_) for v_ in Vb]
    lakv = each(lambda l_, x_: bf(_dot(l_, x_)), lak, Vx)
    Wb = each(lambda t_, a_: bf(_dot(t_, expand(a_))), Tb, Ab)
    U0b = each(lambda t_, l_: bf(_dot(t_, expand(l_))), Tb, lakv)
    mw = each(lambda m_, w_, u_: _dot(m_, jnp.concatenate([expand(w_), expand(u_)], axis=1)),
              mrb, Wb, U0b)
    y0 = each(lambda m_, x_: _dot(m_, x_), mrk, Vx)
    wtb = each(lambda w_, b_: _dot(w_, b_, TN), Wb, Bpb)
    hbd = each(lambda v_, u_, k_, b_: _dot(jnp.concatenate([v_, -u_], axis=0),
                                           jnp.concatenate([k_, b_], axis=0), TN),
               Vb, U0b, Kpb, Bpb)
    for i, (c, q) in enumerate(chains):
        qt_s[c, q] = bf(rh_s[rows(c), lanes(q)] - mw[i][:, :MXU_DIM])
        y_s[rows(c), lanes(q)] = y0[i] - mw[i][:, MXU_DIM:]
        pc = pc_s[c * CHUNK:c * CHUNK + 1, lanes(q)]
        diag = jnp.where(eye_bd, jnp.broadcast_to(pc, (MXU_DIM, MXU_DIM)), 0.0)
        g_s[c, q] = bf(diag - jnp.where(bdm, wtb[i], 0.0))
        hm = jnp.where(bdm, hbd[i], 0.0)
        h_s[c, q] = hm[0:HEAD] + hm[HEAD:2 * HEAD] + hm[2 * HEAD:3 * HEAD] + hm[3 * HEAD:4 * HEAD]

    S = [state[q] for q in range(nquad)]
    for c in range(nchunk):
        for q in range(nquad):
            Sb = bf(S[q])
            y_s[rows(c), lanes(q)] = y_s[rows(c), lanes(q)] + _dot(qt_s[c, q], expand(Sb), NT)
            S[q] = _dot(Sb, g_s[c, q]) + h_s[c, q]
    for q in range(nquad):
        state[q] = S[q]

    y = y_s[...]
    mean = seg(y) * (1.0 / HEAD)
    d = y - mean
    var = seg(d * d) * (1.0 / HEAD)
    yn = d * lax.rsqrt(var + GN_EPS) * lnw + lnb
    out_ref[0] = ((yn + bonus) * _silu(gr_ref[0])).astype(BF16)


def _mm_exact_rhs_left(m, x):
    h1, h2, h3 = _split3(x)
    return _dot(m, h1) + _dot(m, h2) + _dot(m, h3)


def _rwkv_call(proj, v_first, lw):
    B, S, _ = proj.shape
    has_vmix = v_first is not None
    ts = min(S, 256)
    nP = SHIFT_WIDTH
    tok = lambda w, j: pl.BlockSpec((1, ts, w), lambda b, t: (b, t, j))
    full = lambda a: pl.BlockSpec(a.shape, lambda b, t: (0,) * a.ndim)

    args = [proj, proj]
    specs = [tok(nP, C_P // nP), tok(RWKV_WIDTH, C_GR // RWKV_WIDTH)]
    if has_vmix:
        args += [proj, v_first]
        specs += [tok(LANES, C_Y // LANES), tok(RWKV_WIDTH, 0)]
    consts = [lw["mu"], lw["vecs"], lw["wdec"], lw["wicl"]]
    if has_vmix:
        consts += [lw["muy"], lw["wvm"]]
    tid = jnp.arange(ts)
    same = (tid[:, None] // CHUNK) == (tid[None, :] // CHUNK)
    tri = (same & (tid[:, None] >= tid[None, :])).astype(BF16)
    consts += [lw["ones"], tri, same.astype(BF16), lw["bd"]]
    args += consts
    specs += [full(a) for a in consts]

    out_shape = [jax.ShapeDtypeStruct((B, S, RWKV_WIDTH), BF16)]
    out_specs = [tok(RWKV_WIDTH, 0)]
    if not has_vmix:
        out_shape.append(jax.ShapeDtypeStruct((B, S, RWKV_WIDTH), F32))
        out_specs.append(tok(RWKV_WIDTH, 0))

    big = pltpu.VMEM((ts, RWKV_WIDTH), F32)
    scratch = [pltpu.VMEM((1, nP), F32), pltpu.VMEM((1, LANES), F32),
               pltpu.VMEM((RWKV_WIDTH // MXU_DIM, HEAD, MXU_DIM), F32)] + [big] * 3
    nquad = RWKV_WIDTH // MXU_DIM
    scratch += [pltpu.VMEM((7, ts, RWKV_WIDTH), BF16),
                pltpu.VMEM((ts // CHUNK, nquad, MXU_DIM, MXU_DIM), BF16),
                pltpu.VMEM((ts // CHUNK, nquad, HEAD, MXU_DIM), F32),
                pltpu.VMEM((ts // CHUNK, nquad, CHUNK, MXU_DIM), BF16)]
    outs = pl.pallas_call(
        functools.partial(_rwkv_kernel, has_vmix=has_vmix, ts=ts),
        grid=(B, S // ts),
        in_specs=specs,
        out_specs=out_specs,
        out_shape=out_shape,
        scratch_shapes=scratch,
        compiler_params=pltpu.CompilerParams(
            dimension_semantics=("arbitrary", "arbitrary"), vmem_limit_bytes=VMEM_LIMIT),
        name="rwkv_vmix" if has_vmix else "rwkv_first",
    )(*args)
    if has_vmix:
        return outs[0], v_first
    return outs[0], outs[1]


def _rot_half(x, sina, sinb):
    half = MLA_ROPE // 2
    return pltpu.roll(x, LANES - half, 1) * sina + pltpu.roll(x, half, 1) * sinb


def _mla_prep_kernel(cq_ref, ckv_ref, yb_ref, cosq_ref, cosk_ref, sina_ref, sinb_ref,
                     gq_ref, gkv_ref, wq_ref, wkv_ref, q_ref, k_ref, v_ref, *, scale):
    cosq, cosk, sina, sinb = cosq_ref[0], cosk_ref[0], sina_ref[0], sinb_ref[0]
    cqn = _rms(cq_ref[0], gq_ref[...]).astype(BF16)
    qall = _dot(cqn, wq_ref[...])
    ckvn = _rms(ckv_ref[0], gkv_ref[...]).astype(BF16)
    kvall = _dot(ckvn, wkv_ref[...])
    yb = yb_ref[0]
    kr = yb * cosk + _rot_half(yb, sina, sinb)
    for h in range(HEADS):
        qh = qall[:, h * LANES:(h + 1) * LANES]
        q_ref[0, h] = ((qh * cosq + _rot_half(qh, sina, sinb)) * scale).astype(BF16)
        k_ref[0, h] = (kvall[:, h * LANES:(h + 1) * LANES] + kr).astype(BF16)
        v_ref[0, h] = kvall[:, (HEADS + h) * LANES:(HEADS + h + 1) * LANES].astype(BF16)


def _mla_prep_call(proj, tabs, gq, gkv, wq, wkv):
    B, S, _ = proj.shape
    tm = min(S, 256)
    tok = lambda w, j: pl.BlockSpec((1, tm, w), lambda b, t: (b, t, j))
    full = lambda a: pl.BlockSpec(a.shape, lambda b, t: (0,) * a.ndim)
    slab = jax.ShapeDtypeStruct((B, HEADS, S, LANES), BF16)
    slab_spec = pl.BlockSpec((1, HEADS, tm, LANES), lambda b, t: (b, 0, t, 0))
    scale = float(HEAD + MLA_ROPE) ** -0.5 * 1.4426950408889634
    return pl.pallas_call(
        functools.partial(_mla_prep_kernel, scale=scale),
        grid=(B, S // tm),
        in_specs=[tok(Q_LORA, C_Q // Q_LORA), tok(KV_LORA, C_KV // KV_LORA), tok(LANES, C_Y // LANES),
                  tok(LANES, 0), tok(LANES, 0), tok(LANES, 0), tok(LANES, 0),
                  full(gq), full(gkv), full(wq), full(wkv)],
        out_specs=[slab_spec, slab_spec, slab_spec],
        out_shape=[slab, slab, slab],
        compiler_params=pltpu.CompilerParams(
            dimension_semantics=("arbitrary", "arbitrary"), vmem_limit_bytes=VMEM_LIMIT),
        name="mla_prep",
    )(proj, proj, proj, *tabs, gq, gkv, wq, wkv)


ATT_HEADS = 4
ATT_BLOCK = 512


def _attn_kernel(qi_ref, ki_ref, q_ref, k_ref, v_ref, g_ref, o_ref, m_s, l_s, acc_s, *, blk):
    step = pl.program_id(2)
    qi = qi_ref[step]
    ki = ki_ref[step]
    heads = range(ATT_HEADS)

    @pl.when(ki == 0)
    def _():
        m_s[...] = jnp.full_like(m_s, -jnp.inf)
        l_s[...] = jnp.zeros_like(l_s)
        acc_s[...] = jnp.zeros_like(acc_s)

    def update(diagonal):
        if diagonal:
            rr = lax.broadcasted_iota(jnp.int32, (blk, blk), 0)
            cc = lax.broadcasted_iota(jnp.int32, (blk, blk), 1)
            dead = cc > rr
        for h in heads:
            s = _dot(q_ref[0, h], k_ref[0, h], NT)
            if diagonal:
                s = jnp.where(dead, -jnp.inf, s)
            m_prev = m_s[h]
            m_new = jnp.maximum(m_prev, jnp.max(s, axis=1, keepdims=True))
            p = jnp.exp2(s - m_new[:, 0:1])
            alpha = jnp.exp2(m_prev - m_new)
            l_s[h] = alpha * l_s[h] + jnp.sum(p, axis=1, keepdims=True)
            acc_s[h] = alpha * acc_s[h] + _dot(p.astype(BF16), v_ref[0, h])
            m_s[h] = m_new

    @pl.when(ki < qi)
    def _():
        update(False)

    @pl.when(ki == qi)
    def _():
        update(True)
        g = _silu(g_ref[0])
        for j in range(ATT_HEADS // 2):
            o = acc_s[2 * j] / l_s[2 * j] + acc_s[2 * j + 1] / l_s[2 * j + 1]
            o_ref[0, :, j * LANES:(j + 1) * LANES] = (o * g[:, j * LANES:(j + 1) * LANES]).astype(BF16)


def _attn_call(q, k, v, proj):
    B, H, S, _ = q.shape
    blk = min(S, ATT_BLOCK)
    n = S // blk
    pairs = [(i, j) for i in range(n) for j in range(i + 1)]
    qi_tab = jnp.array([i for i, _ in pairs], jnp.int32)
    ki_tab = jnp.array([j for _, j in pairs], jnp.int32)
    wout = ATT_HEADS * HEAD
    gm0 = C_GM // wout
    qspec = pl.BlockSpec((1, ATT_HEADS, blk, LANES), lambda b, g, s, qt, kt: (b, g, qt[s], 0))
    kspec = pl.BlockSpec((1, ATT_HEADS, blk, LANES), lambda b, g, s, qt, kt: (b, g, kt[s], 0))
    grid_spec = pltpu.PrefetchScalarGridSpec(
        num_scalar_prefetch=2,
        grid=(B, H // ATT_HEADS, len(pairs)),
        in_specs=[qspec, kspec, kspec,
                  pl.BlockSpec((1, blk, wout), lambda b, g, s, qt, kt: (b, qt[s], gm0 + g))],
        out_specs=pl.BlockSpec((1, blk, wout), lambda b, g, s, qt, kt: (b, qt[s], g)),
        scratch_shapes=[pltpu.VMEM((ATT_HEADS, blk, LANES), F32)] * 3,
    )
    return pl.pallas_call(
        functools.partial(_attn_kernel, blk=blk),
        grid_spec=grid_spec,
        out_shape=jax.ShapeDtypeStruct((B, S, H * HEAD), BF16),
        compiler_params=pltpu.CompilerParams(
            dimension_semantics=("arbitrary",) * 3, vmem_limit_bytes=VMEM_LIMIT),
        name="mla_attn",
    )(qi_tab, ki_tab, q, k, v, proj)


def _outproj_kernel(yr_ref, ym_ref, x_ref, gate_ref, w_ref, fg_ref, o_ref, *, final):
    y = _dot(yr_ref[0], w_ref[0:RWKV_WIDTH, :]) + _dot(ym_ref[0], w_ref[RWKV_WIDTH:, :])
    xn = x_ref[0] + gate_ref[0] * y
    if final:
        xn = _rms(xn, fg_ref[...])
    o_ref[0] = xn


def _outproj_call(yr, ym, x, gate, w, fg, final):
    B, S, D = x.shape
    tm = min(S, 512)
    tok = lambda w_: pl.BlockSpec((1, tm, w_), lambda b, t: (b, t, 0))
    return pl.pallas_call(
        functools.partial(_outproj_kernel, final=final),
        grid=(B, S // tm),
        in_specs=[tok(RWKV_WIDTH), tok(RWKV_WIDTH), tok(D),
                  pl.BlockSpec((1, 1, D), lambda b, t: (b, 0, 0)),
                  pl.BlockSpec(w.shape, lambda b, t: (0, 0)),
                  pl.BlockSpec((1, D), lambda b, t: (0, 0))],
        out_specs=tok(D),
        out_shape=jax.ShapeDtypeStruct((B, S, D), F32),
        compiler_params=pltpu.CompilerParams(
            dimension_semantics=("arbitrary", "arbitrary"), vmem_limit_bytes=VMEM_LIMIT),
        name="outproj_final" if final else "outproj",
    )(yr, ym, x, gate.reshape(B, 1, D), w, fg.reshape(1, D))


def _hilo(w):
    hi = w.astype(BF16)
    lo = (w - hi.astype(F32)).astype(BF16)
    return jnp.stack([hi, lo])


def _pack_w_in(w_in_l, w_vmix_down_l):
    D = w_in_l.shape[0]
    z = lambda n: jnp.zeros((D, n), F32)
    o_kr = SHIFT_WIDTH + RWKV_WIDTH + Q_LORA + KV_LORA
    o_cq = SHIFT_WIDTH + RWKV_WIDTH
    o_ckv = o_cq + Q_LORA
    o_gm = o_kr + MLA_ROPE
    vm = z(VMIX_LORA) if w_vmix_down_l is None else w_vmix_down_l
    yblk = jnp.concatenate([vm, z(KR_LANE - VMIX_LORA), w_in_l[:, o_kr:o_gm],
                            z(LANES - KR_LANE - MLA_ROPE)], axis=1)
    cols = [w_in_l[:, :SHIFT_WIDTH], yblk, w_in_l[:, o_ckv:o_kr],
            w_in_l[:, SHIFT_WIDTH:o_cq], w_in_l[:, o_gm:IN_WIDTH], w_in_l[:, o_cq:o_ckv]]
    return jnp.concatenate(cols, axis=1).astype(BF16)


def _pack_wq(w_uq_l):
    w = w_uq_l.reshape(Q_LORA, HEADS, HEAD + MLA_ROPE)
    w = jnp.pad(w, ((0, 0), (0, 0), (0, LANES - HEAD - MLA_ROPE)))
    return w.reshape(Q_LORA, HEADS * LANES).astype(BF16)


def _pack_wkv(w_ukv_l):
    w = w_ukv_l.reshape(KV_LORA, HEADS, 2 * HEAD)
    wk = jnp.pad(w[:, :, :HEAD], ((0, 0), (0, 0), (0, LANES - HEAD)))
    wv = w[:, :, HEAD:]
    zero = jnp.zeros_like(wv)
    even = jnp.concatenate([wv, zero], axis=-1)
    odd = jnp.concatenate([zero, wv], axis=-1)
    is_odd = (jnp.arange(HEADS) % 2 == 1)[None, :, None]
    wv2 = jnp.where(is_odd, odd, even)
    return jnp.concatenate([wk.reshape(KV_LORA, -1), wv2.reshape(KV_LORA, -1)], axis=1).astype(BF16)


def _pad_rows(w, lo, total):
    return jnp.pad(w, ((lo, total - lo - w.shape[0]), (0, 0)))


def kernel(x, c, positions, norm_g, w_ada, b_ada, w_in, w_vmix_down, mu_shift, mu_vmix, w0, w_decay_up, a0, w_iclr_up, v0, w_vmix_up, k_k, k_a, r_k, lnx_w, lnx_b, q_norm_g, kv_norm_g, w_uq, w_ukv, w_out, final_g):
    B, S, D = x.shape
    L = w_in.shape[0]
    assert D == D_MODEL and S % CHUNK == 0

    mod = _mod_call(c, w_ada, b_ada)
    tabs = _rope_call(positions)

    hid = jnp.arange(RWKV_WIDTH) // HEAD
    ones = (hid[:, None] == hid[None, :]).astype(BF16)
    qid = jnp.arange(MXU_DIM) // HEAD
    bd = (qid[:, None] == qid[None, :]).astype(BF16)

    v_first = None
    for l in range(L):
        shift, scale, gate = mod[l, :, :D], mod[l, :, D:2 * D], mod[l, :, 2 * D:]
        w_in_p = _pack_w_in(w_in[l], None if l == 0 else w_vmix_down[l - 1])
        proj = _inproj_call(x, norm_g[l], scale, shift, w_in_p)

        vecs = jnp.stack([w0[l], a0[l], k_k[l], k_a[l], r_k[l].reshape(-1), lnx_w[l], lnx_b[l],
                          v0[l - 1] if l > 0 else jnp.zeros((RWKV_WIDTH,), F32)])
        lw = {
            "mu": mu_shift[l].reshape(1, SHIFT_WIDTH),
            "vecs": vecs,
            "wdec": _hilo(_pad_rows(w_decay_up[l], 0, LANES)),
            "wicl": _hilo(_pad_rows(w_iclr_up[l], DECAY_LORA, LANES)),
            "ones": ones, "bd": bd,
        }
        if l > 0:
            lw["muy"] = jnp.pad(mu_vmix[l - 1], (0, LANES - VMIX_LORA)).reshape(1, LANES)
            lw["wvm"] = _hilo(_pad_rows(w_vmix_up[l - 1], 0, LANES))
        y_rwkv, v_first = _rwkv_call(proj, v_first, lw)

        q, k, v = _mla_prep_call(proj, tabs, q_norm_g[l].reshape(1, Q_LORA),
                                 kv_norm_g[l].reshape(1, KV_LORA), _pack_wq(w_uq[l]), _pack_wkv(w_ukv[l]))
        y_mla = _attn_call(q, k, v, proj)

        x = _outproj_call(y_rwkv, y_mla, x, gate, w_out[l].astype(BF16), final_g, final=(l == L - 1))
    return x
```

```python
import functools

import jax
import jax.numpy as jnp
from jax import lax
from jax.experimental import pallas as pl
from jax.experimental.pallas import tpu as pltpu

F32 = jnp.float32
BF16 = jnp.bfloat16

D_MODEL = 1024
RWKV_WIDTH = 512
HEAD = 64
HEADS = 8
DECAY_LORA = 64
ICLR_LORA = 64
VMIX_LORA = 32
MLA_ROPE = 32
Q_LORA = 384
KV_LORA = 256
ROPE_THETA = 10000.0
NORM_EPS = 1e-6
GN_EPS = 64e-5
SHIFT_WIDTH = 3 * RWKV_WIDTH + DECAY_LORA + ICLR_LORA
IN_WIDTH = 3360

LANES = 128
MXU_DIM = 256
QUAD = MXU_DIM // HEAD
VMEM_LIMIT = 48 * 1024 * 1024

C_P = 0
C_Y = SHIFT_WIDTH
C_KV = C_Y + LANES
C_GR = C_KV + KV_LORA
C_GM = C_GR + RWKV_WIDTH
C_Q = C_GM + RWKV_WIDTH
P_WIDTH = C_Q + Q_LORA
KR_LANE = 64

CHUNK = 64

NN = ((1,), (0,))
NT = ((1,), (1,))
TN = ((0,), (0,))


def _dot(a, b, dims=NN):
    return lax.dot_general(a, b, (dims, ((), ())), preferred_element_type=F32)


def _split(x):
    hi = x.astype(BF16)
    lo = (x - hi.astype(F32)).astype(BF16)
    return hi, lo


def _mm(a, b, dims=NN):
    return _dot(a[0], b[0], dims) + _dot(a[0], b[1], dims) + _dot(a[1], b[0], dims)


def _rms(x, g):
    return x * lax.rsqrt(jnp.mean(x * x, axis=-1, keepdims=True) + NORM_EPS) * g


def _silu(x):
    return x * jax.nn.sigmoid(x)


def _mod_kernel(c_ref, w_ref, b_ref, o_ref):
    ca = _silu(c_ref[...])
    o_ref[0] = _mm(_split(ca), _split(w_ref[0])) + b_ref[0]


def _mod_call(c, w_ada, b_ada):
    L, D, D3 = w_ada.shape
    B = c.shape[0]
    nb = D3 // D
    return pl.pallas_call(
        _mod_kernel,
        grid=(L, nb),
        in_specs=[
            pl.BlockSpec((B, D), lambda l, j: (0, 0)),
            pl.BlockSpec((1, D, D), lambda l, j: (l, 0, j)),
            pl.BlockSpec((1, 1, D), lambda l, j: (l, 0, j)),
        ],
        out_specs=pl.BlockSpec((1, B, D), lambda l, j: (l, 0, j)),
        out_shape=jax.ShapeDtypeStruct((L, B, D3), F32),
        compiler_params=pltpu.CompilerParams(
            dimension_semantics=("arbitrary", "arbitrary"), vmem_limit_bytes=VMEM_LIMIT),
        name="adaln_mod",
    )(c, w_ada, b_ada.reshape(L, 1, D3))


def _rope_kernel(pos_ref, inv_ref, cosq_ref, cosk_ref, sina_ref, sinb_ref):
    ang = pos_ref[0].astype(F32) * inv_ref[...]
    cs = jnp.cos(ang)
    sn = jnp.sin(ang)
    lane = lax.broadcasted_iota(jnp.int32, ang.shape, 1)
    half = MLA_ROPE // 2
    in_rope = (lane >= KR_LANE) & (lane < KR_LANE + MLA_ROPE)
    first = (lane >= KR_LANE) & (lane < KR_LANE + half)
    second = (lane >= KR_LANE + half) & (lane < KR_LANE + MLA_ROPE)
    cosk = jnp.where(in_rope, cs, 0.0)
    cosk_ref[0] = cosk
    cosq_ref[0] = jnp.where(lane < KR_LANE, 1.0, cosk)
    sina_ref[0] = jnp.where(first, -sn, 0.0)
    sinb_ref[0] = jnp.where(second, sn, 0.0)


def _rope_call(positions):
    B, S = positions.shape
    ts = min(S, 512)
    half = MLA_ROPE // 2
    inv = ROPE_THETA ** (-jnp.arange(0, MLA_ROPE, 2, dtype=F32) / MLA_ROPE)
    inv_full = jnp.zeros((1, LANES), F32)
    inv_full = inv_full.at[0, KR_LANE:KR_LANE + half].set(inv)
    inv_full = inv_full.at[0, KR_LANE + half:KR_LANE + MLA_ROPE].set(inv)
    tab = jax.ShapeDtypeStruct((B, S, LANES), F32)
    spec = pl.BlockSpec((1, ts, LANES), lambda b, t: (b, t, 0))
    return pl.pallas_call(
        _rope_kernel,
        grid=(B, S // ts),
        in_specs=[pl.BlockSpec((1, ts, 1), lambda b, t: (b, t, 0)),
                  pl.BlockSpec((1, LANES), lambda b, t: (0, 0))],
        out_specs=[spec, spec, spec, spec],
        out_shape=[tab, tab, tab, tab],
        compiler_params=pltpu.CompilerParams(
            dimension_semantics=("arbitrary", "arbitrary"), vmem_limit_bytes=VMEM_LIMIT),
        name="rope_tables",
    )(positions.reshape(B, S, 1), inv_full)


def _inproj_kernel(x_ref, g_ref, sc_ref, sh_ref, w_ref, o_ref):
    h = _rms(x_ref[0], g_ref[...]) * (1.0 + sc_ref[0]) + sh_ref[0]
    o_ref[0] = _dot(h.astype(BF16), w_ref[...])


def _inproj_call(x, g, scale, shift, w):
    B, S, D = x.shape
    N = w.shape[1]
    tm = min(S, 256)
    return pl.pallas_call(
        _inproj_kernel,
        grid=(B, S // tm),
        in_specs=[
            pl.BlockSpec((1, tm, D), lambda b, t: (b, t, 0)),
            pl.BlockSpec((1, D), lambda b, t: (0, 0)),
            pl.BlockSpec((1, 1, D), lambda b, t: (b, 0, 0)),
            pl.BlockSpec((1, 1, D), lambda b, t: (b, 0, 0)),
            pl.BlockSpec((D, N), lambda b, t: (0, 0)),
        ],
        out_specs=pl.BlockSpec((1, tm, N), lambda b, t: (b, t, 0)),
        out_shape=jax.ShapeDtypeStruct((B, S, N), F32),
        compiler_params=pltpu.CompilerParams(
            dimension_semantics=("arbitrary", "arbitrary"), vmem_limit_bytes=VMEM_LIMIT),
        name="inproj",
    )(x, g.reshape(1, D), scale.reshape(B, 1, D), shift.reshape(B, 1, D), w)


def _rwkv_kernel(*refs, has_vmix, ts):
    it = iter(refs)
    p_ref = next(it)
    gr_ref = next(it)
    if has_vmix:
        yb_ref = next(it)
        vf_ref = next(it)
    mu_ref = next(it)
    vecs_ref = next(it)
    wdec_ref = next(it)
    wicl_ref = next(it)
    if has_vmix:
        muy_ref = next(it)
        wvm_ref = next(it)
    tri_ref = next(it)
    bd_ref = next(it)
    out_ref = next(it)
    if not has_vmix:
        vf_out_ref = next(it)
    carry_p = next(it)
    carry_y = next(it)
    state = next(it)
    rh_s, pc_s, y_s, ops_s, g_s, h_s, qt_s = (next(it) for _ in range(7))

    t = pl.program_id(1)

    @pl.when(t == 0)
    def _():
        carry_p[...] = jnp.zeros_like(carry_p)
        carry_y[...] = jnp.zeros_like(carry_y)
        state[...] = jnp.zeros_like(state)

    w0 = vecs_ref[0:1, :]
    a0 = vecs_ref[1:2, :]
    k_k = vecs_ref[2:3, :]
    k_a = vecs_ref[3:4, :]
    r_k = vecs_ref[4:5, :]
    lnw = vecs_ref[5:6, :]
    lnb = vecs_ref[6:7, :]
    bd = bd_ref[...]

    def seg(x):
        parts = []
        for q in range(RWKV_WIDTH // MXU_DIM):
            hi, lo = _split(x[:, q * MXU_DIM:(q + 1) * MXU_DIM])
            parts.append(_dot(hi, bd) + _dot(lo, bd))
        return jnp.concatenate(parts, axis=1)

    p = p_ref[0]
    row = lax.broadcasted_iota(jnp.int32, (ts, 1), 0)
    prev = jnp.where(row == 0, carry_p[...], pltpu.roll(p, 1, 0))
    carry_p[...] = p[ts - 1:ts, :]
    ps = p + (prev - p) * mu_ref[...]
    r = ps[:, 0:RWKV_WIDTH]
    k = ps[:, RWKV_WIDTH:2 * RWKV_WIDTH]
    v = ps[:, 2 * RWKV_WIDTH:3 * RWKV_WIDTH]
    xl = ps[:, 3 * RWKV_WIDTH:SHIFT_WIDTH]

    dec = w0 + _dot(jnp.tanh(xl).astype(BF16), wdec_ref[...])
    z = -dec
    w_log = -(jnp.maximum(z, 0.0) + jnp.log1p(jnp.exp(-jnp.abs(z)))) - 0.5
    lw = -jnp.exp(w_log)
    a = jax.nn.sigmoid(a0 + _dot(xl.astype(BF16), wicl_ref[...]))

    if has_vmix:
        yb = yb_ref[0]
        prevy = jnp.where(row == 0, carry_y[...], pltpu.roll(yb, 1, 0))
        carry_y[...] = yb[ts - 1:ts, :]
        ys = yb + (prevy - yb) * muy_ref[...]
        v0 = vecs_ref[7:8, :]
        mix = jax.nn.sigmoid(v0 + _dot(ys.astype(BF16), wvm_ref[...]))
        v = v + (vf_ref[0] - v) * mix
    else:
        vf_out_ref[0] = v

    kk = k * k_k
    kk = kk / jnp.maximum(jnp.sqrt(seg(kk * kk)), 1e-12)
    k2 = k * (1.0 + (a - 1.0) * k_a)
    bonus = seg(r * k2 * r_k) * v

    lw_hi, lw_lo = _split(lw)
    tri = tri_ref[...]
    cl = _dot(tri, lw_hi) + _dot(tri, lw_lo)
    ce = jnp.concatenate(
        [jnp.broadcast_to(cl[(c + 1) * CHUNK - 1:(c + 1) * CHUNK, :], (CHUNK, RWKV_WIDTH))
         for c in range(ts // CHUNK)], axis=0)
    e_out = jnp.exp(-cl)
    pce = jnp.exp(ce)
    rh = r * jnp.exp(cl)
    kh = k2 * e_out
    bh = kk * a * e_out
    bf = lambda x_: x_.astype(BF16)
    rh_s[...] = rh
    pc_s[...] = pce
    ops_s[0] = bf(kk * jnp.exp(cl - lw))
    ops_s[1] = bf(rh)
    ops_s[2] = bf(bh)
    ops_s[3] = bf(kh)
    ops_s[4] = bf(v)
    ops_s[5] = bf(kh * pce)
    ops_s[6] = bf(bh * pce)

    ri = lax.broadcasted_iota(jnp.int32, (CHUNK, MXU_DIM), 0)
    ci = lax.broadcasted_iota(jnp.int32, (CHUNK, MXU_DIM), 1) & (HEAD - 1)
    strict = ci < ri
    incl = ci <= ri
    eye_sb = jnp.where(ci == ri, 1.0, 0.0).astype(F32)
    er = lax.broadcasted_iota(jnp.int32, (MXU_DIM, MXU_DIM), 0)
    ec = lax.broadcasted_iota(jnp.int32, (MXU_DIM, MXU_DIM), 1)
    eye_bd = er == ec
    bdm = (er // HEAD) == (ec // HEAD)

    def expand(xb):
        return jnp.concatenate([xb] * QUAD, axis=0) * bd

    nchunk = ts // CHUNK
    nquad = RWKV_WIDTH // MXU_DIM
    chains = [(c, q) for c in range(nchunk) for q in range(nquad)]
    rows = lambda c: slice(c * CHUNK, (c + 1) * CHUNK)
    lanes = lambda q: slice(q * MXU_DIM, (q + 1) * MXU_DIM)
    op = lambda j: [ops_s[j, rows(c), lanes(q)] for c, q in chains]
    each = lambda f, *ls: [f(*xs) for xs in zip(*ls)]

    Ab, Rb, Bhb, Khb, Vb, Kpb, Bpb = (op(j) for j in range(7))
    AR = each(lambda a_, r_: jnp.concatenate([a_, r_], axis=0), Ab, Rb)
    sb = each(lambda x_, b_: _dot(x_, expand(b_), NT), AR, Bhb)
    sk = each(lambda x_, k_: _dot(x_, expand(k_), NT), AR, Khb)
    lab = [jnp.where(strict, x_[:CHUNK], 0.0) for x_ in sb]
    mrb = [bf(jnp.where(incl, x_[CHUNK:], 0.0)) for x_ in sb]
    lak = [bf(jnp.where(strict, x_[:CHUNK], 0.0)) for x_ in sk]
    mrk = [bf(jnp.where(incl, x_[CHUNK:], 0.0)) for x_ in sk]
    T = [eye_sb - l_ for l_ in lab]
    P = lab
    Px = [expand(bf(p_)) for p_ in P]
    n = 2
    while n < CHUNK:
        P = each(lambda p_, x_: _dot(bf(p_), x_), P, Px)
        Px = [expand(bf(p_)) for p_ in P]
        T = each(lambda t_, x_: t_ + _dot(bf(t_), x_), T, Px)
        n *= 2
    Tb = [bf(t_) for t_ in T]
    Vx = [expand(v_) for v_ in Vb]
    lakv = each(lambda l_, x_: bf(_dot(l_, x_)), lak, Vx)
    Wb = each(lambda t_, a_: bf(_dot(t_, expand(a_))), Tb, Ab)
    U0b = each(lambda t_, l_: bf(_dot(t_, expand(l_))), Tb, lakv)
    mw = each(lambda m_, w_, u_: _dot(m_, jnp.concatenate([expand(w_), expand(u_)], axis=1)),
              mrb, Wb, U0b)
    y0 = each(lambda m_, x_: _dot(m_, x_), mrk, Vx)
    wtb = each(lambda w_, b_: _dot(w_, b_, TN), Wb, Bpb)
    hbd = each(lambda v_, u_, k_, b_: _dot(jnp.concatenate([v_, -u_], axis=0),
                                           jnp.concatenate([k_, b_], axis=0), TN),
               Vb, U0b, Kpb, Bpb)
    for i, (c, q) in enumerate(chains):
        qt_s[c, q] = bf(rh_s[rows(c), lanes(q)] - mw[i][:, :MXU_DIM])
        y_s[rows(c), lanes(q)] = y0[i] - mw[i][:, MXU_DIM:]
        pc = pc_s[c * CHUNK:c * CHUNK + 1, lanes(q)]
        diag = jnp.where(eye_bd, jnp.broadcast_to(pc, (MXU_DIM, MXU_DIM)), 0.0)
        g_s[c, q] = bf(diag - jnp.where(bdm, wtb[i], 0.0))
        hm = jnp.where(bdm, hbd[i], 0.0)
        h_s[c, q] = hm[0:HEAD] + hm[HEAD:2 * HEAD] + hm[2 * HEAD:3 * HEAD] + hm[3 * HEAD:4 * HEAD]

    S = [state[q] for q in range(nquad)]
    for c in range(nchunk):
        for q in range(nquad):
            Sb = bf(S[q])
            y_s[rows(c), lanes(q)] = y_s[rows(c), lanes(q)] + _dot(qt_s[c, q], expand(Sb), NT)
            S[q] = _dot(Sb, g_s[c, q]) + h_s[c, q]
    for q in range(nquad):
        state[q] = S[q]

    y = y_s[...]
    mean = seg(y) * (1.0 / HEAD)
    d = y - mean
    var = seg(d * d) * (1.0 / HEAD)
    yn = d * lax.rsqrt(var + GN_EPS) * lnw + lnb
    out_ref[0] = ((yn + bonus) * _silu(gr_ref[0])).astype(BF16)


def _rwkv_call(proj, v_first, lw):
    B, S, _ = proj.shape
    has_vmix = v_first is not None
    ts = min(S, 256)
    nP = SHIFT_WIDTH
    tok = lambda w, j: pl.BlockSpec((1, ts, w), lambda b, t: (b, t, j))
    full = lambda a: pl.BlockSpec(a.shape, lambda b, t: (0,) * a.ndim)

    args = [proj, proj]
    specs = [tok(nP, C_P // nP), tok(RWKV_WIDTH, C_GR // RWKV_WIDTH)]
    if has_vmix:
        args += [proj, v_first]
        specs += [tok(LANES, C_Y // LANES), tok(RWKV_WIDTH, 0)]
    consts = [lw["mu"], lw["vecs"], lw["wdec"], lw["wicl"]]
    if has_vmix:
        consts += [lw["muy"], lw["wvm"]]
    tid = jnp.arange(ts)
    same_chunk = (tid[:, None] // CHUNK) == (tid[None, :] // CHUNK)
    tri = (same_chunk & (tid[:, None] >= tid[None, :])).astype(BF16)
    consts += [tri, lw["bd"]]
    args += consts
    specs += [full(a) for a in consts]

    out_shape = [jax.ShapeDtypeStruct((B, S, RWKV_WIDTH), BF16)]
    out_specs = [tok(RWKV_WIDTH, 0)]
    if not has_vmix:
        out_shape.append(jax.ShapeDtypeStruct((B, S, RWKV_WIDTH), F32))
        out_specs.append(tok(RWKV_WIDTH, 0))

    big = pltpu.VMEM((ts, RWKV_WIDTH), F32)
    scratch = [pltpu.VMEM((1, nP), F32), pltpu.VMEM((1, LANES), F32),
               pltpu.VMEM((RWKV_WIDTH // MXU_DIM, HEAD, MXU_DIM), F32)] + [big] * 3
    nquad = RWKV_WIDTH // MXU_DIM
    scratch += [pltpu.VMEM((7, ts, RWKV_WIDTH), BF16),
                pltpu.VMEM((ts // CHUNK, nquad, MXU_DIM, MXU_DIM), BF16),
                pltpu.VMEM((ts // CHUNK, nquad, HEAD, MXU_DIM), F32),
                pltpu.VMEM((ts // CHUNK, nquad, CHUNK, MXU_DIM), BF16)]
    outs = pl.pallas_call(
        functools.partial(_rwkv_kernel, has_vmix=has_vmix, ts=ts),
        grid=(B, S // ts),
        in_specs=specs,
        out_specs=out_specs,
        out_shape=out_shape,
        scratch_shapes=scratch,
        compiler_params=pltpu.CompilerParams(
            dimension_semantics=("arbitrary", "arbitrary"), vmem_limit_bytes=VMEM_LIMIT),
        name="rwkv_vmix" if has_vmix else "rwkv_first",
    )(*args)
    if has_vmix:
        return outs[0], v_first
    return outs[0], outs[1]


def _rot_half(x, sina, sinb):
    half = MLA_ROPE // 2
    return pltpu.roll(x, LANES - half, 1) * sina + pltpu.roll(x, half, 1) * sinb


def _mla_prep_kernel(cq_ref, ckv_ref, yb_ref, cosq_ref, cosk_ref, sina_ref, sinb_ref,
                     gq_ref, gkv_ref, wq_ref, wkv_ref, q_ref, k_ref, v_ref, *, scale):
    cosq, cosk, sina, sinb = cosq_ref[0], cosk_ref[0], sina_ref[0], sinb_ref[0]
    cqn = _rms(cq_ref[0], gq_ref[...]).astype(BF16)
    qall = _dot(cqn, wq_ref[...])
    ckvn = _rms(ckv_ref[0], gkv_ref[...]).astype(BF16)
    kvall = _dot(ckvn, wkv_ref[...])
    yb = yb_ref[0]
    kr = yb * cosk + _rot_half(yb, sina, sinb)
    lane = lax.broadcasted_iota(jnp.int32, (1, LANES), 1)
    for h in range(HEADS):
        qh = qall[:, h * LANES:(h + 1) * LANES]
        q_ref[0, h] = ((qh * cosq + _rot_half(qh, sina, sinb)) * scale).astype(BF16)
        k_ref[0, h] = (kvall[:, h * LANES:(h + 1) * LANES] + kr).astype(BF16)
        ones_lane = jnp.where(lane == (HEAD if h % 2 == 0 else 0), 1.0, 0.0)
        v_ref[0, h] = (kvall[:, (HEADS + h) * LANES:(HEADS + h + 1) * LANES] + ones_lane).astype(BF16)


def _mla_prep_call(proj, tabs, gq, gkv, wq, wkv):
    B, S, _ = proj.shape
    tm = min(S, 256)
    tok = lambda w, j: pl.BlockSpec((1, tm, w), lambda b, t: (b, t, j))
    full = lambda a: pl.BlockSpec(a.shape, lambda b, t: (0,) * a.ndim)
    slab = jax.ShapeDtypeStruct((B, HEADS, S, LANES), BF16)
    slab_spec = pl.BlockSpec((1, HEADS, tm, LANES), lambda b, t: (b, 0, t, 0))
    scale = float(HEAD + MLA_ROPE) ** -0.5 * 1.4426950408889634
    return pl.pallas_call(
        functools.partial(_mla_prep_kernel, scale=scale),
        grid=(B, S // tm),
        in_specs=[tok(Q_LORA, C_Q // Q_LORA), tok(KV_LORA, C_KV // KV_LORA), tok(LANES, C_Y // LANES),
                  tok(LANES, 0), tok(LANES, 0), tok(LANES, 0), tok(LANES, 0),
                  full(gq), full(gkv), full(wq), full(wkv)],
        out_specs=[slab_spec, slab_spec, slab_spec],
        out_shape=[slab, slab, slab],
        compiler_params=pltpu.CompilerParams(
            dimension_semantics=("arbitrary", "arbitrary"), vmem_limit_bytes=VMEM_LIMIT),
        name="mla_prep",
    )(proj, proj, proj, *tabs, gq, gkv, wq, wkv)


ATT_HEADS = 8
ATT_BLOCK = 512
ATT_QSPLIT = 1


def _attn_kernel(qi_ref, ki_ref, q_ref, k_ref, v_ref, g_ref, o_ref, m_s, acc_s, *, blk):
    step = pl.program_id(2)
    qi = qi_ref[step]
    ki = ki_ref[step]
    heads = range(ATT_HEADS)

    @pl.when(ki == 0)
    def _():
        m_s[...] = jnp.full_like(m_s, -jnp.inf)
        acc_s[...] = jnp.zeros_like(acc_s)

    def update(diagonal):
        qw = blk // ATT_QSPLIT
        units = [(h, c) for h in heads for c in range(ATT_QSPLIT)]
        cols = lambda c: slice(c * qw, (c + 1) * qw)
        if diagonal:
            kidx = lax.broadcasted_iota(jnp.int32, (blk, qw), 0)
            qidx = lax.broadcasted_iota(jnp.int32, (blk, qw), 1)
        score = lambda u: _dot(k_ref[0, u[0]], q_ref[0, u[0], cols(u[1]), :], NT)
        st_next = score(units[0])
        for i, (h, c) in enumerate(units):
            st = st_next
            if i + 1 < len(units):
                st_next = score(units[i + 1])
            if diagonal:
                st = jnp.where(kidx > qidx + c * qw, -jnp.inf, st)
            m_prev = m_s[h, :, cols(c)]
            m_new = jnp.maximum(m_prev, jnp.max(st, axis=0, keepdims=True))
            pt = jnp.exp2(st - m_new).astype(BF16)
            alpha = jnp.exp2(m_prev - m_new)
            acc_s[h, :, cols(c)] = alpha * acc_s[h, :, cols(c)] + _dot(v_ref[0, h], pt, TN)
            m_s[h, :, cols(c)] = m_new

    @pl.when(ki < qi)
    def _():
        update(False)

    @pl.when(ki == qi)
    def _():
        update(True)
        g = _silu(g_ref[0])
        upper = lax.broadcasted_iota(jnp.int32, (LANES, blk), 0) < HEAD
        for j in range(ATT_HEADS // 2):
            even, odd = acc_s[2 * j], acc_s[2 * j + 1]
            ot = jnp.where(upper, even / even[HEAD:HEAD + 1, :], odd / odd[0:1, :])
            o_ref[0, :, j * LANES:(j + 1) * LANES] = (ot.T * g[:, j * LANES:(j + 1) * LANES]).astype(BF16)


def _attn_call(q, k, v, proj):
    B, H, S, _ = q.shape
    blk = min(S, ATT_BLOCK)
    n = S // blk
    pairs = [(i, j) for i in range(n) for j in range(i + 1)]
    qi_tab = jnp.array([i for i, _ in pairs], jnp.int32)
    ki_tab = jnp.array([j for _, j in pairs], jnp.int32)
    wout = ATT_HEADS * HEAD
    gm0 = C_GM // wout
    qspec = pl.BlockSpec((1, ATT_HEADS, blk, LANES), lambda b, g, s, qt, kt: (b, g, qt[s], 0))
    kspec = pl.BlockSpec((1, ATT_HEADS, blk, LANES), lambda b, g, s, qt, kt: (b, g, kt[s], 0))
    grid_spec = pltpu.PrefetchScalarGridSpec(
        num_scalar_prefetch=2,
        grid=(B, H // ATT_HEADS, len(pairs)),
        in_specs=[qspec, kspec, kspec,
                  pl.BlockSpec((1, blk, wout), lambda b, g, s, qt, kt: (b, qt[s], gm0 + g))],
        out_specs=pl.BlockSpec((1, blk, wout), lambda b, g, s, qt, kt: (b, qt[s], g)),
        scratch_shapes=[pltpu.VMEM((ATT_HEADS, 1, blk), F32), pltpu.VMEM((ATT_HEADS, LANES, blk), F32)],
    )
    return pl.pallas_call(
        functools.partial(_attn_kernel, blk=blk),
        grid_spec=grid_spec,
        out_shape=jax.ShapeDtypeStruct((B, S, H * HEAD), BF16),
        compiler_params=pltpu.CompilerParams(
            dimension_semantics=("arbitrary",) * 3, vmem_limit_bytes=VMEM_LIMIT),
        name="mla_attn",
    )(qi_tab, ki_tab, q, k, v, proj)


def _outproj_kernel(yr_ref, ym_ref, x_ref, gate_ref, w_ref, fg_ref, o_ref, *, final):
    y = _dot(yr_ref[0], w_ref[0:RWKV_WIDTH, :]) + _dot(ym_ref[0], w_ref[RWKV_WIDTH:, :])
    xn = x_ref[0] + gate_ref[0] * y
    if final:
        xn = _rms(xn, fg_ref[...])
    o_ref[0] = xn


def _outproj_call(yr, ym, x, gate, w, fg, final):
    B, S, D = x.shape
    tm = min(S, 512)
    tok = lambda w_: pl.BlockSpec((1, tm, w_), lambda b, t: (b, t, 0))
    return pl.pallas_call(
        functools.partial(_outproj_kernel, final=final),
        grid=(B, S // tm),
        in_specs=[tok(RWKV_WIDTH), tok(RWKV_WIDTH), tok(D),
                  pl.BlockSpec((1, 1, D), lambda b, t: (b, 0, 0)),
                  pl.BlockSpec(w.shape, lambda b, t: (0, 0)),
                  pl.BlockSpec((1, D), lambda b, t: (0, 0))],
        out_specs=tok(D),
        out_shape=jax.ShapeDtypeStruct((B, S, D), F32),
        compiler_params=pltpu.CompilerParams(
            dimension_semantics=("arbitrary", "arbitrary"), vmem_limit_bytes=VMEM_LIMIT),
        name="outproj_final" if final else "outproj",
    )(yr, ym, x, gate.reshape(B, 1, D), w, fg.reshape(1, D))


def _pack_w_in(w_in_l, w_vmix_down_l):
    D = w_in_l.shape[0]
    z = lambda n: jnp.zeros((D, n), F32)
    o_kr = SHIFT_WIDTH + RWKV_WIDTH + Q_LORA + KV_LORA
    o_cq = SHIFT_WIDTH + RWKV_WIDTH
    o_ckv = o_cq + Q_LORA
    o_gm = o_kr + MLA_ROPE
    vm = z(VMIX_LORA) if w_vmix_down_l is None else w_vmix_down_l
    yblk = jnp.concatenate([vm, z(KR_LANE - VMIX_LORA), w_in_l[:, o_kr:o_gm],
                            z(LANES - KR_LANE - MLA_ROPE)], axis=1)
    cols = [w_in_l[:, :SHIFT_WIDTH], yblk, w_in_l[:, o_ckv:o_kr],
            w_in_l[:, SHIFT_WIDTH:o_cq], w_in_l[:, o_gm:IN_WIDTH], w_in_l[:, o_cq:o_ckv]]
    return jnp.concatenate(cols, axis=1).astype(BF16)


def _pack_wq(w_uq_l):
    w = w_uq_l.reshape(Q_LORA, HEADS, HEAD + MLA_ROPE)
    w = jnp.pad(w, ((0, 0), (0, 0), (0, LANES - HEAD - MLA_ROPE)))
    return w.reshape(Q_LORA, HEADS * LANES).astype(BF16)


def _pack_wkv(w_ukv_l):
    w = w_ukv_l.reshape(KV_LORA, HEADS, 2 * HEAD)
    wk = jnp.pad(w[:, :, :HEAD], ((0, 0), (0, 0), (0, LANES - HEAD)))
    wv = w[:, :, HEAD:]
    zero = jnp.zeros_like(wv)
    even = jnp.concatenate([wv, zero], axis=-1)
    odd = jnp.concatenate([zero, wv], axis=-1)
    is_odd = (jnp.arange(HEADS) % 2 == 1)[None, :, None]
    wv2 = jnp.where(is_odd, odd, even)
    return jnp.concatenate([wk.reshape(KV_LORA, -1), wv2.reshape(KV_LORA, -1)], axis=1).astype(BF16)


def _pad_rows(w, lo, total):
    return jnp.pad(w, ((lo, total - lo - w.shape[0]), (0, 0)))


def kernel(x, c, positions, norm_g, w_ada, b_ada, w_in, w_vmix_down, mu_shift, mu_vmix, w0, w_decay_up, a0, w_iclr_up, v0, w_vmix_up, k_k, k_a, r_k, lnx_w, lnx_b, q_norm_g, kv_norm_g, w_uq, w_ukv, w_out, final_g):
    B, S, D = x.shape
    L = w_in.shape[0]
    assert D == D_MODEL and S % CHUNK == 0

    mod = _mod_call(c, w_ada, b_ada)
    tabs = _rope_call(positions)

    qid = jnp.arange(MXU_DIM) // HEAD
    bd = (qid[:, None] == qid[None, :]).astype(BF16)

    v_first = None
    for l in range(L):
        shift, scale, gate = mod[l, :, :D], mod[l, :, D:2 * D], mod[l, :, 2 * D:]
        w_in_p = _pack_w_in(w_in[l], None if l == 0 else w_vmix_down[l - 1])
        proj = _inproj_call(x, norm_g[l], scale, shift, w_in_p)

        vecs = jnp.stack([w0[l], a0[l], k_k[l], k_a[l], r_k[l].reshape(-1), lnx_w[l], lnx_b[l],
                          v0[l - 1] if l > 0 else jnp.zeros((RWKV_WIDTH,), F32)])
        lw = {
            "mu": mu_shift[l].reshape(1, SHIFT_WIDTH),
            "vecs": vecs,
            "wdec": _pad_rows(w_decay_up[l], 0, LANES).astype(BF16),
            "wicl": _pad_rows(w_iclr_up[l], DECAY_LORA, LANES).astype(BF16),
            "bd": bd,
        }
        if l > 0:
            lw["muy"] = jnp.pad(mu_vmix[l - 1], (0, LANES - VMIX_LORA)).reshape(1, LANES)
            lw["wvm"] = _pad_rows(w_vmix_up[l - 1], 0, LANES).astype(BF16)
        y_rwkv, v_first = _rwkv_call(proj, v_first, lw)

        q, k, v = _mla_prep_call(proj, tabs, q_norm_g[l].reshape(1, Q_LORA),
                                 kv_norm_g[l].reshape(1, KV_LORA), _pack_wq(w_uq[l]), _pack_wkv(w_ukv[l]))
        y_mla = _attn_call(q, k, v, proj)

        x = _outproj_call(y_rwkv, y_mla, x, gate, w_out[l].astype(BF16), final_g, final=(l == L - 1))
    return x
```

```python
import functools

import jax
import jax.numpy as jnp
from jax import lax
from jax.experimental import pallas as pl
from jax.experimental.pallas import tpu as pltpu

F32 = jnp.float32
BF16 = jnp.bfloat16

D_MODEL = 1024
RWKV_WIDTH = 512
HEAD = 64
HEADS = 8
DECAY_LORA = 64
ICLR_LORA = 64
VMIX_LORA = 32
MLA_ROPE = 32
Q_LORA = 384
KV_LORA = 256
ROPE_THETA = 10000.0
NORM_EPS = 1e-6
GN_EPS = 64e-5
SHIFT_WIDTH = 3 * RWKV_WIDTH + DECAY_LORA + ICLR_LORA
IN_WIDTH = 3360

LANES = 128
MXU_DIM = 256
QUAD = MXU_DIM // HEAD
VMEM_LIMIT = 48 * 1024 * 1024

C_P = 0
C_Y = SHIFT_WIDTH
C_KV = C_Y + LANES
C_GR = C_KV + KV_LORA
C_GM = C_GR + RWKV_WIDTH
C_Q = C_GM + RWKV_WIDTH
P_WIDTH = C_Q + Q_LORA
KR_LANE = 64

CHUNK = 64
DECAY_SCALE = 0.6065306597126334

PROJ_TILE = 512
RWKV_TILE = 256

NN = ((1,), (0,))
NT = ((1,), (1,))
TN = ((0,), (0,))


def _dot(a, b, dims=NN):
    return lax.dot_general(a, b, (dims, ((), ())), preferred_element_type=F32)


def _split(x):
    hi = x.astype(BF16)
    lo = (x - hi.astype(F32)).astype(BF16)
    return hi, lo


def _mm(a, b, dims=NN):
    return _dot(a[0], b[0], dims) + _dot(a[0], b[1], dims) + _dot(a[1], b[0], dims)


def _rms(x, g):
    return x * lax.rsqrt(jnp.mean(x * x, axis=-1, keepdims=True) + NORM_EPS) * g


def _sigmoid(x):
    return 0.5 * jnp.tanh(0.5 * x) + 0.5


def _silu(x):
    return x * _sigmoid(x)


def _mod_kernel(c_ref, w_ref, b_ref, o_ref):
    ca = _silu(c_ref[...])
    o_ref[0] = _mm(_split(ca), _split(w_ref[0])) + b_ref[0]


def _mod_call(c, w_ada, b_ada):
    L, D, D3 = w_ada.shape
    B = c.shape[0]
    nb = D3 // D
    return pl.pallas_call(
        _mod_kernel,
        grid=(L, nb),
        in_specs=[
            pl.BlockSpec((B, D), lambda l, j: (0, 0)),
            pl.BlockSpec((1, D, D), lambda l, j: (l, 0, j)),
            pl.BlockSpec((1, 1, D), lambda l, j: (l, 0, j)),
        ],
        out_specs=pl.BlockSpec((1, B, D), lambda l, j: (l, 0, j)),
        out_shape=jax.ShapeDtypeStruct((L, B, D3), F32),
        compiler_params=pltpu.CompilerParams(
            dimension_semantics=("arbitrary", "arbitrary"), vmem_limit_bytes=VMEM_LIMIT),
        name="adaln_mod",
    )(c, w_ada, b_ada.reshape(L, 1, D3))


def _rope_kernel(pos_ref, inv_ref, cosq_ref, cosk_ref, sina_ref, sinb_ref):
    ang = pos_ref[0].astype(F32) * inv_ref[...]
    cs = jnp.cos(ang)
    sn = jnp.sin(ang)
    lane = lax.broadcasted_iota(jnp.int32, ang.shape, 1)
    half = MLA_ROPE // 2
    in_rope = (lane >= KR_LANE) & (lane < KR_LANE + MLA_ROPE)
    first = (lane >= KR_LANE) & (lane < KR_LANE + half)
    second = (lane >= KR_LANE + half) & (lane < KR_LANE + MLA_ROPE)
    cosk = jnp.where(in_rope, cs, 0.0)
    cosk_ref[0] = cosk
    cosq_ref[0] = jnp.where(lane < KR_LANE, 1.0, cosk)
    sina_ref[0] = jnp.where(first, -sn, 0.0)
    sinb_ref[0] = jnp.where(second, sn, 0.0)


def _rope_call(positions):
    B, S = positions.shape
    ts = min(S, PROJ_TILE)
    half = MLA_ROPE // 2
    inv = ROPE_THETA ** (-jnp.arange(0, MLA_ROPE, 2, dtype=F32) / MLA_ROPE)
    inv_full = jnp.zeros((1, LANES), F32)
    inv_full = inv_full.at[0, KR_LANE:KR_LANE + half].set(inv)
    inv_full = inv_full.at[0, KR_LANE + half:KR_LANE + MLA_ROPE].set(inv)
    tab = jax.ShapeDtypeStruct((B, S, LANES), F32)
    spec = pl.BlockSpec((1, ts, LANES), lambda b, t: (b, t, 0))
    return pl.pallas_call(
        _rope_kernel,
        grid=(B, S // ts),
        in_specs=[pl.BlockSpec((1, ts, 1), lambda b, t: (b, t, 0)),
                  pl.BlockSpec((1, LANES), lambda b, t: (0, 0))],
        out_specs=[spec, spec, spec, spec],
        out_shape=[tab, tab, tab, tab],
        compiler_params=pltpu.CompilerParams(
            dimension_semantics=("arbitrary", "arbitrary"), vmem_limit_bytes=VMEM_LIMIT),
        name="rope_tables",
    )(positions.reshape(B, S, 1), inv_full)


def _inproj_kernel(x_ref, g_ref, sc_ref, sh_ref, w_ref, o_ref):
    h = _rms(x_ref[0], g_ref[...]) * (1.0 + sc_ref[0]) + sh_ref[0]
    o_ref[0] = _dot(h.astype(BF16), w_ref[...])


def _inproj_call(x, g, scale, shift, w):
    B, S, D = x.shape
    N = w.shape[1]
    tm = min(S, PROJ_TILE)
    return pl.pallas_call(
        _inproj_kernel,
        grid=(B, S // tm),
        in_specs=[
            pl.BlockSpec((1, tm, D), lambda b, t: (b, t, 0)),
            pl.BlockSpec((1, D), lambda b, t: (0, 0)),
            pl.BlockSpec((1, 1, D), lambda b, t: (b, 0, 0)),
            pl.BlockSpec((1, 1, D), lambda b, t: (b, 0, 0)),
            pl.BlockSpec((D, N), lambda b, t: (0, 0)),
        ],
        out_specs=pl.BlockSpec((1, tm, N), lambda b, t: (b, t, 0)),
        out_shape=jax.ShapeDtypeStruct((B, S, N), F32),
        compiler_params=pltpu.CompilerParams(
            dimension_semantics=("arbitrary", "arbitrary"), vmem_limit_bytes=VMEM_LIMIT),
        name="inproj",
    )(x, g.reshape(1, D), scale.reshape(B, 1, D), shift.reshape(B, 1, D), w)


def _rwkv_kernel(*refs, has_vmix, ts):
    it = iter(refs)
    p_ref = next(it)
    gr_ref = next(it)
    if has_vmix:
        yb_ref = next(it)
        vf_ref = next(it)
    mu_ref = next(it)
    vecs_ref = next(it)
    wdec_ref = next(it)
    wicl_ref = next(it)
    if has_vmix:
        muy_ref = next(it)
        wvm_ref = next(it)
    tri_ref = next(it)
    bd_ref = next(it)
    out_ref = next(it)
    if not has_vmix:
        vf_out_ref = next(it)
    carry_p = next(it)
    carry_y = next(it)
    state = next(it)
    rh_s, pc_s, y_s, ops_s, g_s, h_s, qt_s = (next(it) for _ in range(7))

    t = pl.program_id(1)

    @pl.when(t == 0)
    def _():
        carry_p[...] = jnp.zeros_like(carry_p)
        carry_y[...] = jnp.zeros_like(carry_y)
        state[...] = jnp.zeros_like(state)

    w0 = vecs_ref[0:1, :]
    a0 = vecs_ref[1:2, :]
    k_k = vecs_ref[2:3, :]
    k_a = vecs_ref[3:4, :]
    r_k = vecs_ref[4:5, :]
    lnw = vecs_ref[5:6, :]
    lnb = vecs_ref[6:7, :]
    bd = bd_ref[...]

    def seg(x):
        xb = x.astype(BF16)
        return jnp.concatenate([_dot(xb[:, q * MXU_DIM:(q + 1) * MXU_DIM], bd)
                                for q in range(RWKV_WIDTH // MXU_DIM)], axis=1)

    p = p_ref[0]
    row = lax.broadcasted_iota(jnp.int32, (ts, 1), 0)
    prev = jnp.where(row == 0, carry_p[...], pltpu.roll(p, 1, 0))
    carry_p[...] = p[ts - 1:ts, :]
    ps = p + (prev - p) * mu_ref[...]
    r = ps[:, 0:RWKV_WIDTH]
    k = ps[:, RWKV_WIDTH:2 * RWKV_WIDTH]
    v = ps[:, 2 * RWKV_WIDTH:3 * RWKV_WIDTH]
    xl = ps[:, 3 * RWKV_WIDTH:SHIFT_WIDTH]

    dec = w0 + _dot(jnp.tanh(xl).astype(BF16), wdec_ref[...])
    lw = -DECAY_SCALE * _sigmoid(dec)
    a = _sigmoid(a0 + _dot(xl.astype(BF16), wicl_ref[...]))

    if has_vmix:
        yb = yb_ref[0]
        prevy = jnp.where(row == 0, carry_y[...], pltpu.roll(yb, 1, 0))
        carry_y[...] = yb[ts - 1:ts, :]
        ys = yb + (prevy - yb) * muy_ref[...]
        v0 = vecs_ref[7:8, :]
        mix = _sigmoid(v0 + _dot(ys.astype(BF16), wvm_ref[...]))
        v = v + (vf_ref[0] - v) * mix
    else:
        vf_out_ref[0] = v

    kk = k * k_k
    kk = kk * lax.rsqrt(jnp.maximum(seg(kk * kk), 1e-24))
    k2 = k * (1.0 + (a - 1.0) * k_a)
    bonus = seg(r * k2 * r_k) * v

    lw_hi, lw_lo = _split(lw)
    tri = tri_ref[...]
    cl = _dot(tri, lw_hi) + _dot(tri, lw_lo)
    ce = jnp.concatenate(
        [jnp.broadcast_to(cl[(c + 1) * CHUNK - 1:(c + 1) * CHUNK, :], (CHUNK, RWKV_WIDTH))
         for c in range(ts // CHUNK)], axis=0)
    e_out = jnp.exp(-cl)
    pce = jnp.exp(ce)
    rh = r * jnp.exp(cl)
    kh = k2 * e_out
    bh = kk * a * e_out
    bf = lambda x_: x_.astype(BF16)
    rh_s[...] = rh
    pc_s[...] = pce
    ops_s[0] = bf(kk * jnp.exp(cl - lw))
    ops_s[1] = bf(rh)
    ops_s[2] = bf(bh)
    ops_s[3] = bf(kh)
    ops_s[4] = bf(v)
    ops_s[5] = bf(kh * pce)
    ops_s[6] = bf(bh * pce)

    ri = lax.broadcasted_iota(jnp.int32, (CHUNK, MXU_DIM), 0)
    lane_head = lax.broadcasted_iota(jnp.int32, (CHUNK, MXU_DIM), 1) // HEAD
    ci = lax.broadcasted_iota(jnp.int32, (CHUNK, MXU_DIM), 1) & (HEAD - 1)
    strict = ci < ri
    incl = ci <= ri
    eye_sb = jnp.where(ci == ri, 1.0, 0.0).astype(F32)
    er = lax.broadcasted_iota(jnp.int32, (MXU_DIM, MXU_DIM), 0)
    ec = lax.broadcasted_iota(jnp.int32, (MXU_DIM, MXU_DIM), 1)
    eye_bd = er == ec
    bdm = (er // HEAD) == (ec // HEAD)

    def expand(xb):
        return jnp.concatenate([xb] * QUAD, axis=0) * bd

    nchunk = ts // CHUNK
    nquad = RWKV_WIDTH // MXU_DIM
    chains = [(c, q) for c in range(nchunk) for q in range(nquad)]
    rows = lambda c: slice(c * CHUNK, (c + 1) * CHUNK)
    lanes = lambda q: slice(q * MXU_DIM, (q + 1) * MXU_DIM)
    op = lambda j: [ops_s[j, rows(c), lanes(q)] for c, q in chains]
    each = lambda f, *ls: [f(*xs) for xs in zip(*ls)]

    Ab, Rb, Bhb, Khb, Vb, Kpb, Bpb = (op(j) for j in range(7))
    AR = each(lambda a_, r_: jnp.concatenate([a_, r_], axis=0), Ab, Rb)
    sb = each(lambda x_, b_: _dot(x_, expand(b_), NT), AR, Bhb)
    sk = each(lambda x_, k_: _dot(x_, expand(k_), NT), AR, Khb)
    lab = [jnp.where(strict, x_[:CHUNK], 0.0) for x_ in sb]
    mrb = [bf(jnp.where(incl, x_[CHUNK:], 0.0)) for x_ in sb]
    lak = [bf(jnp.where(strict, x_[:CHUNK], 0.0)) for x_ in sk]
    mrk = [bf(jnp.where(incl, x_[CHUNK:], 0.0)) for x_ in sk]
    T = [eye_sb - l_ for l_ in lab]
    P = each(lambda l_: _dot(bf(l_), expand(bf(l_))), lab)
    n = 4
    while n < CHUNK:
        tp = each(lambda t_, p_: _dot(jnp.concatenate([bf(t_), bf(p_)], axis=0), expand(bf(p_))), T, P)
        T = each(lambda t_, x_: t_ + x_[:CHUNK], T, tp)
        P = [x_[CHUNK:] for x_ in tp]
        n *= 2
    Tb = each(lambda t_, p_: bf(t_ + _dot(bf(t_), expand(bf(p_)))), T, P)
    lm = each(lambda l_, m_, v_: _dot(jnp.concatenate([l_, m_], axis=0), expand(v_)), lak, mrk, Vb)
    lakv = [bf(x_[:CHUNK]) for x_ in lm]
    y0 = [x_[CHUNK:] for x_ in lm]
    Wb = each(lambda t_, a_: bf(_dot(t_, expand(a_))), Tb, Ab)
    U0b = each(lambda t_, l_: bf(_dot(t_, expand(l_))), Tb, lakv)
    mw = each(lambda m_, w_, u_: _dot(m_, jnp.concatenate([expand(w_), expand(u_)], axis=1)),
              mrb, Wb, U0b)
    wtb = each(lambda w_, b_: _dot(w_, b_, TN), Wb, Bpb)
    hbd = each(lambda v_, u_, k_, b_: _dot(jnp.concatenate([v_, -u_], axis=0),
                                           jnp.concatenate([k_, b_], axis=0), TN),
               Vb, U0b, Kpb, Bpb)
    for i, (c, q) in enumerate(chains):
        qt_s[c, q] = bf(rh_s[rows(c), lanes(q)] - mw[i][:, :MXU_DIM])
        y_s[rows(c), lanes(q)] = y0[i] - mw[i][:, MXU_DIM:]
        pc = pc_s[c * CHUNK:c * CHUNK + 1, lanes(q)]
        diag = jnp.where(eye_bd, jnp.broadcast_to(pc, (MXU_DIM, MXU_DIM)), 0.0)
        g_s[c, q] = bf(diag - jnp.where(bdm, wtb[i], 0.0))
        hm = hbd[i][0:HEAD]
        for h in range(1, QUAD):
            hm = jnp.where(lane_head == h, hbd[i][h * HEAD:(h + 1) * HEAD], hm)
        h_s[c, q] = hm

    S = [state[q] for q in range(nquad)]
    for c in range(nchunk):
        for q in range(nquad):
            Sb = bf(S[q])
            y_s[rows(c), lanes(q)] = y_s[rows(c), lanes(q)] + _dot(qt_s[c, q], expand(Sb), NT)
            S[q] = _dot(Sb, g_s[c, q]) + h_s[c, q]
    for q in range(nquad):
        state[q] = S[q]

    y = y_s[...]
    mean = seg(y) * (1.0 / HEAD)
    d = y - mean
    var = seg(d * d) * (1.0 / HEAD)
    yn = d * lax.rsqrt(var + GN_EPS) * lnw + lnb
    out_ref[0] = ((yn + bonus) * _silu(gr_ref[0])).astype(BF16)


def _rwkv_call(proj, v_first, lw):
    B, S, _ = proj.shape
    has_vmix = v_first is not None
    ts = min(S, RWKV_TILE)
    nP = SHIFT_WIDTH
    tok = lambda w, j: pl.BlockSpec((1, ts, w), lambda b, t: (b, t, j))
    full = lambda a: pl.BlockSpec(a.shape, lambda b, t: (0,) * a.ndim)

    args = [proj, proj]
    specs = [tok(nP, C_P // nP), tok(RWKV_WIDTH, C_GR // RWKV_WIDTH)]
    if has_vmix:
        args += [proj, v_first]
        specs += [tok(LANES, C_Y // LANES), tok(RWKV_WIDTH, 0)]
    consts = [lw["mu"], lw["vecs"], lw["wdec"], lw["wicl"]]
    if has_vmix:
        consts += [lw["muy"], lw["wvm"]]
    tid = jnp.arange(ts)
    same_chunk = (tid[:, None] // CHUNK) == (tid[None, :] // CHUNK)
    tri = (same_chunk & (tid[:, None] >= tid[None, :])).astype(BF16)
    consts += [tri, lw["bd"]]
    args += consts
    specs += [full(a) for a in consts]

    out_shape = [jax.ShapeDtypeStruct((B, S, RWKV_WIDTH), BF16)]
    out_specs = [tok(RWKV_WIDTH, 0)]
    if not has_vmix:
        out_shape.append(jax.ShapeDtypeStruct((B, S, RWKV_WIDTH), F32))
        out_specs.append(tok(RWKV_WIDTH, 0))

    big = pltpu.VMEM((ts, RWKV_WIDTH), F32)
    scratch = [pltpu.VMEM((1, nP), F32), pltpu.VMEM((1, LANES), F32),
               pltpu.VMEM((RWKV_WIDTH // MXU_DIM, HEAD, MXU_DIM), F32)] + [big] * 3
    nquad = RWKV_WIDTH // MXU_DIM
    scratch += [pltpu.VMEM((7, ts, RWKV_WIDTH), BF16),
                pltpu.VMEM((ts // CHUNK, nquad, MXU_DIM, MXU_DIM), BF16),
                pltpu.VMEM((ts // CHUNK, nquad, HEAD, MXU_DIM), F32),
                pltpu.VMEM((ts // CHUNK, nquad, CHUNK, MXU_DIM), BF16)]
    outs = pl.pallas_call(
        functools.partial(_rwkv_kernel, has_vmix=has_vmix, ts=ts),
        grid=(B, S // ts),
        in_specs=specs,
        out_specs=out_specs,
        out_shape=out_shape,
        scratch_shapes=scratch,
        compiler_params=pltpu.CompilerParams(
            dimension_semantics=("arbitrary", "arbitrary"), vmem_limit_bytes=VMEM_LIMIT),
        name="rwkv_vmix" if has_vmix else "rwkv_first",
    )(*args)
    if has_vmix:
        return outs[0], v_first
    return outs[0], outs[1]


def _rot_half(x, sina, sinb):
    half = MLA_ROPE // 2
    return pltpu.roll(x, LANES - half, 1) * sina + pltpu.roll(x, half, 1) * sinb


def _mla_prep_kernel(cq_ref, ckv_ref, yb_ref, cosq_ref, cosk_ref, sina_ref, sinb_ref,
                     gq_ref, gkv_ref, wq_ref, wkv_ref, q_ref, k_ref, v_ref, *, scale):
    cosq, cosk, sina, sinb = cosq_ref[0], cosk_ref[0], sina_ref[0], sinb_ref[0]
    cqn = _rms(cq_ref[0], gq_ref[...]).astype(BF16)
    qall = _dot(cqn, wq_ref[...])
    ckvn = _rms(ckv_ref[0], gkv_ref[...]).astype(BF16)
    kvall = _dot(ckvn, wkv_ref[...])
    yb = yb_ref[0]
    kr = yb * cosk + _rot_half(yb, sina, sinb)
    lane = lax.broadcasted_iota(jnp.int32, (1, LANES), 1)
    for h in range(HEADS):
        qh = qall[:, h * LANES:(h + 1) * LANES]
        q_ref[0, h] = ((qh * cosq + _rot_half(qh, sina, sinb)) * scale).astype(BF16)
        k_ref[0, h] = (kvall[:, h * LANES:(h + 1) * LANES] + kr).astype(BF16)
        ones_lane = jnp.where(lane == (HEAD if h % 2 == 0 else 0), 1.0, 0.0)
        v_ref[0, h] = (kvall[:, (HEADS + h) * LANES:(HEADS + h + 1) * LANES] + ones_lane).astype(BF16)


def _mla_prep_call(proj, tabs, gq, gkv, wq, wkv):
    B, S, _ = proj.shape
    tm = min(S, PROJ_TILE)
    tok = lambda w, j: pl.BlockSpec((1, tm, w), lambda b, t: (b, t, j))
    full = lambda a: pl.BlockSpec(a.shape, lambda b, t: (0,) * a.ndim)
    slab =jax.ShapeDtypeStruct((B, HEADS, S, LANES), BF16)
    slab_spec = pl.BlockSpec((1, HEADS, tm, LANES), lambda b, t: (b, 0, t, 0))
    scale = float(HEAD + MLA_ROPE) ** -0.5 * 1.4426950408889634
    return pl.pallas_call(
        functools.partial(_mla_prep_kernel, scale=scale),
        grid=(B, S // tm),
        in_specs=[tok(Q_LORA, C_Q // Q_LORA), tok(KV_LORA, C_KV // KV_LORA), tok(LANES, C_Y // LANES),
                  tok(LANES, 0), tok(LANES, 0), tok(LANES, 0), tok(LANES, 0),
                  full(gq), full(gkv), full(wq), full(wkv)],
        out_specs=[slab_spec, slab_spec, slab_spec],
        out_shape=[slab, slab, slab],
        compiler_params=pltpu.CompilerParams(
            dimension_semantics=("arbitrary", "arbitrary"), vmem_limit_bytes=VMEM_LIMIT),
        name="mla_prep",
    )(proj, proj, proj, *tabs, gq, gkv, wq, wkv)


ATT_HEADS = 8
ATT_BLOCK = 512


def _attn_kernel(qi_ref, ki_ref, q_ref, k_ref, v_ref, g_ref, o_ref, m_s, acc_s, *, blk):
    step = pl.program_id(2)
    qi = qi_ref[step]
    ki = ki_ref[step]
    heads = range(ATT_HEADS)

    @pl.when(ki == 0)
    def _():
        m_s[...] = jnp.full_like(m_s, -jnp.inf)
        acc_s[...] = jnp.zeros_like(acc_s)

    def update(diagonal):
        if diagonal:
            kidx = lax.broadcasted_iota(jnp.int32, (blk, blk), 0)
            qidx = lax.broadcasted_iota(jnp.int32, (blk, blk), 1)
            dead = kidx > qidx
        score = lambda h_: _dot(k_ref[0, h_], q_ref[0, h_], NT)

        def accumulate(h_, pt_, alpha_):
            acc_s[h_] = alpha_ * acc_s[h_] + _dot(v_ref[0, h_], pt_, TN)

        st_next = score(0)
        pending = None
        for h in heads:
            st = st_next
            if h + 1 < ATT_HEADS:
                st_next = score(h + 1)
            if pending is not None:
                accumulate(*pending)
            if diagonal:
                st = jnp.where(dead, -jnp.inf, st)
            m_prev = m_s[h]
            m_new = jnp.maximum(m_prev, jnp.max(st, axis=0, keepdims=True))
            pt = jnp.exp2(st - m_new).astype(BF16)
            alpha = jnp.exp2(m_prev - m_new)
            m_s[h] = m_new
            pending = (h, pt, alpha)
        accumulate(*pending)

    @pl.when(ki < qi)
    def _():
        update(False)

    @pl.when(ki == qi)
    def _():
        update(True)
        g = _silu(g_ref[0])
        upper = lax.broadcasted_iota(jnp.int32, (LANES, blk), 0) < HEAD
        for j in range(ATT_HEADS // 2):
            even, odd = acc_s[2 * j], acc_s[2 * j + 1]
            ot = jnp.where(upper, even / even[HEAD:HEAD + 1, :], odd / odd[0:1, :])
            o_ref[0, :, j * LANES:(j + 1) * LANES] = (ot.T * g[:, j * LANES:(j + 1) * LANES]).astype(BF16)


def _attn_call(q, k, v, proj):
    B, H, S, _ = q.shape
    blk = min(S, ATT_BLOCK)
    n = S // blk
    pairs = [(i, j) for i in range(n) for j in range(i + 1)]
    qi_tab = jnp.array([i for i, _ in pairs], jnp.int32)
    ki_tab = jnp.array([j for _, j in pairs], jnp.int32)
    wout = ATT_HEADS * HEAD
    gm0 = C_GM // wout
    qspec = pl.BlockSpec((1, ATT_HEADS, blk, LANES), lambda b, g, s, qt, kt: (b, g, qt[s], 0))
    kspec = pl.BlockSpec((1, ATT_HEADS, blk, LANES), lambda b, g, s, qt, kt: (b, g, kt[s], 0))
    grid_spec = pltpu.PrefetchScalarGridSpec(
        num_scalar_prefetch=2,
        grid=(B, H // ATT_HEADS, len(pairs)),
        in_specs=[qspec, kspec, kspec,
                  pl.BlockSpec((1, blk, wout), lambda b, g, s, qt, kt: (b, qt[s], gm0 + g))],
        out_specs=pl.BlockSpec((1, blk, wout), lambda b, g, s, qt, kt: (b, qt[s], g)),
        scratch_shapes=[pltpu.VMEM((ATT_HEADS, 1, blk), F32), pltpu.VMEM((ATT_HEADS, LANES, blk), F32)],
    )
    return pl.pallas_call(
        functools.partial(_attn_kernel, blk=blk),
        grid_spec=grid_spec,
        out_shape=jax.ShapeDtypeStruct((B, S, H * HEAD), BF16),
        compiler_params=pltpu.CompilerParams(
            dimension_semantics=("arbitrary",) * 3, vmem_limit_bytes=VMEM_LIMIT),
        name="mla_attn",
    )(qi_tab, ki_tab, q, k, v, proj)


def _outproj_kernel(yr_ref, ym_ref, x_ref, gate_ref, w_ref, fg_ref, o_ref, *, final):
    y = _dot(yr_ref[0], w_ref[0:RWKV_WIDTH, :]) + _dot(ym_ref[0], w_ref[RWKV_WIDTH:, :])
    xn = x_ref[0] + gate_ref[0] * y
    if final:
        xn = _rms(xn, fg_ref[...])
    o_ref[0] = xn


def _outproj_call(yr, ym, x, gate, w, fg, final):
    B, S, D = x.shape
    tm = min(S, PROJ_TILE)
    tok = lambda w_: pl.BlockSpec((1, tm, w_), lambda b, t: (b, t, 0))
    return pl.pallas_call(
        functools.partial(_outproj_kernel, final=final),
        grid=(B, S // tm),
        in_specs=[tok(RWKV_WIDTH), tok(RWKV_WIDTH), tok(D),
                  pl.BlockSpec((1, 1, D), lambda b, t: (b, 0, 0)),
                  pl.BlockSpec(w.shape, lambda b, t: (0, 0)),
                  pl.BlockSpec((1, D), lambda b, t: (0, 0))],
        out_specs=tok(D),
        out_shape=jax.ShapeDtypeStruct((B, S, D), F32),
        compiler_params=pltpu.CompilerParams(
            dimension_semantics=("arbitrary", "arbitrary"), vmem_limit_bytes=VMEM_LIMIT),
        name="outproj_final" if final else "outproj",
    )(yr, ym, x, gate.reshape(B, 1, D), w, fg.reshape(1, D))


def _pack_w_in(w_in_l, w_vmix_down_l):
    D = w_in_l.shape[0]
    z = lambda n: jnp.zeros((D, n), F32)
    o_kr = SHIFT_WIDTH + RWKV_WIDTH + Q_LORA + KV_LORA
    o_cq = SHIFT_WIDTH + RWKV_WIDTH
    o_ckv = o_cq + Q_LORA
    o_gm = o_kr + MLA_ROPE
    vm = z(VMIX_LORA) if w_vmix_down_l is None else w_vmix_down_l
    yblk = jnp.concatenate([vm, z(KR_LANE - VMIX_LORA), w_in_l[:, o_kr:o_gm],
                            z(LANES - KR_LANE - MLA_ROPE)], axis=1)
    cols = [w_in_l[:, :SHIFT_WIDTH], yblk, w_in_l[:, o_ckv:o_kr],
            w_in_l[:, SHIFT_WIDTH:o_cq], w_in_l[:, o_gm:IN_WIDTH], w_in_l[:, o_cq:o_ckv]]
    return jnp.concatenate(cols, axis=1).astype(BF16)


def _pack_wq(w_uq_l):
    w = w_uq_l.reshape(Q_LORA, HEADS, HEAD + MLA_ROPE)
    w = jnp.pad(w, ((0, 0), (0, 0), (0, LANES - HEAD - MLA_ROPE)))
    return w.reshape(Q_LORA, HEADS * LANES).astype(BF16)


def _pack_wkv(w_ukv_l):
    w = w_ukv_l.reshape(KV_LORA, HEADS, 2 * HEAD)
    wk = jnp.pad(w[:, :, :HEAD], ((0, 0), (0, 0), (0, LANES - HEAD)))
    wv = w[:, :, HEAD:]
    zero = jnp.zeros_like(wv)
    even = jnp.concatenate([wv, zero], axis=-1)
    odd = jnp.concatenate([zero, wv], axis=-1)
    is_odd = (jnp.arange(HEADS) % 2 == 1)[None, :, None]
    wv2 = jnp.where(is_odd, odd, even)
    return jnp.concatenate([wk.reshape(KV_LORA, -1), wv2.reshape(KV_LORA, -1)], axis=1).astype(BF16)


def _pad_rows(w, lo, total):
    return jnp.pad(w, ((lo, total - lo - w.shape[0]), (0, 0)))


def kernel(x, c, positions, norm_g, w_ada, b_ada, w_in, w_vmix_down, mu_shift, mu_vmix, w0, w_decay_up, a0, w_iclr_up, v0, w_vmix_up, k_k, k_a, r_k, lnx_w, lnx_b, q_norm_g, kv_norm_g, w_uq, w_ukv, w_out, final_g):
    B, S, D = x.shape
    L = w_in.shape[0]
    assert D == D_MODEL and S % CHUNK == 0

    mod = _mod_call(c, w_ada, b_ada)
    tabs = _rope_call(positions)

    qid = jnp.arange(MXU_DIM) // HEAD
    bd = (qid[:, None] == qid[None, :]).astype(BF16)

    v_first = None
    for l in range(L):
        shift, scale, gate = mod[l, :, :D], mod[l, :, D:2 * D], mod[l, :, 2 * D:]
        w_in_p = _pack_w_in(w_in[l], None if l == 0 else w_vmix_down[l - 1])
        proj = _inproj_call(x, norm_g[l], scale, shift, w_in_p)

        vecs = jnp.stack([w0[l], a0[l], k_k[l], k_a[l], r_k[l].reshape(-1), lnx_w[l], lnx_b[l],
                          v0[l - 1] if l > 0 else jnp.zeros((RWKV_WIDTH,), F32)])
        lw = {
            "mu": mu_shift[l].reshape(1, SHIFT_WIDTH),
            "vecs": vecs,
            "wdec": _pad_rows(w_decay_up[l], 0, LANES).astype(BF16),
            "wicl": _pad_rows(w_iclr_up[l], DECAY_LORA, LANES).astype(BF16),
            "bd": bd,
        }
        if l > 0:
            lw["muy"] = jnp.pad(mu_vmix[l - 1], (0, LANES - VMIX_LORA)).reshape(1, LANES)
            lw["wvm"] = _pad_rows(w_vmix_up[l - 1], 0, LANES).astype(BF16)
        y_rwkv, v_first = _rwkv_call(proj, v_first, lw)

        q, k, v = _mla_prep_call(proj, tabs, q_norm_g[l].reshape(1, Q_LORA),
                                 kv_norm_g[l].reshape(1, KV_LORA), _pack_wq(w_uq[l]), _pack_wkv(w_ukv[l]))
        y_mla = _attn_call(q, k, v, proj)

        x = _outproj_call(y_rwkv, y_mla, x, gate, w_out[l].astype(BF16), final_g, final=(l == L - 1))
    return x
```

```python
import functools

import jax
import jax.numpy as jnp
from jax import lax
from jax.experimental import pallas as pl
from jax.experimental.pallas import tpu as pltpu

F32 = jnp.float32
BF16 = jnp.bfloat16

D_MODEL = 1024
RWKV_WIDTH = 512
HEAD = 64
HEADS = 8
DECAY_LORA = 64
ICLR_LORA = 64
VMIX_LORA = 32
MLA_ROPE = 32
Q_LORA = 384
KV_LORA = 256
ROPE_THETA = 10000.0
NORM_EPS = 1e-6
GN_EPS = 64e-5
SHIFT_WIDTH = 3 * RWKV_WIDTH + DECAY_LORA + ICLR_LORA
IN_WIDTH = 3360

LANES = 128
MXU_DIM = 256
QUAD = MXU_DIM // HEAD
VMEM_LIMIT = 48 * 1024 * 1024

C_P = 0
C_Y = SHIFT_WIDTH
C_KV = C_Y + LANES
C_GR = C_KV + KV_LORA
C_GM = C_GR + RWKV_WIDTH
C_Q = C_GM + RWKV_WIDTH
P_WIDTH = C_Q + Q_LORA
KR_LANE = 64

CHUNK = 64
DECAY_SCALE = 0.6065306597126334

PROJ_TILE = 512
RWKV_TILE = 512
RWKV_GROUP = 256

NN = ((1,), (0,))
NT = ((1,), (1,))
TN = ((0,), (0,))


def _dot(a, b, dims=NN):
    return lax.dot_general(a, b, (dims, ((), ())), preferred_element_type=F32)


def _split(x):
    hi = x.astype(BF16)
    lo = (x - hi.astype(F32)).astype(BF16)
    return hi, lo


def _mm(a, b, dims=NN):
    return _dot(a[0], b[0], dims) + _dot(a[0], b[1], dims) + _dot(a[1], b[0], dims)


def _rms(x, g):
    return x * lax.rsqrt(jnp.mean(x * x, axis=-1, keepdims=True) + NORM_EPS) * g


def _sigmoid(x):
    return 0.5 * jnp.tanh(0.5 * x) + 0.5


def _silu(x):
    return x * _sigmoid(x)


def _mod_kernel(c_ref, w_ref, b_ref, o_ref):
    ca = _silu(c_ref[...])
    o_ref[0] = _mm(_split(ca), _split(w_ref[0])) + b_ref[0]


def _mod_call(c, w_ada, b_ada):
    L, D, D3 = w_ada.shape
    B = c.shape[0]
    nb = D3 // D
    return pl.pallas_call(
        _mod_kernel,
        grid=(L, nb),
        in_specs=[
            pl.BlockSpec((B, D), lambda l, j: (0, 0)),
            pl.BlockSpec((1, D, D), lambda l, j: (l, 0, j)),
            pl.BlockSpec((1, 1, D), lambda l, j: (l, 0, j)),
        ],
        out_specs=pl.BlockSpec((1, B, D), lambda l, j: (l, 0, j)),
        out_shape=jax.ShapeDtypeStruct((L, B, D3), F32),
        compiler_params=pltpu.CompilerParams(
            dimension_semantics=("arbitrary", "arbitrary"), vmem_limit_bytes=VMEM_LIMIT),
        name="adaln_mod",
    )(c, w_ada, b_ada.reshape(L, 1, D3))


def _rope_kernel(pos_ref, inv_ref, cosq_ref, cosk_ref, sina_ref, sinb_ref):
    ang = pos_ref[0].astype(F32) * inv_ref[...]
    cs = jnp.cos(ang)
    sn = jnp.sin(ang)
    lane = lax.broadcasted_iota(jnp.int32, ang.shape, 1)
    half = MLA_ROPE // 2
    in_rope = (lane >= KR_LANE) & (lane < KR_LANE + MLA_ROPE)
    first = (lane >= KR_LANE) & (lane < KR_LANE + half)
    second = (lane >= KR_LANE + half) & (lane < KR_LANE + MLA_ROPE)
    cosk = jnp.where(in_rope, cs, 0.0)
    cosk_ref[0] = cosk
    cosq_ref[0] = jnp.where(lane < KR_LANE, 1.0, cosk)
    sina_ref[0] = jnp.where(first, -sn, 0.0)
    sinb_ref[0] = jnp.where(second, sn, 0.0)


def _rope_call(positions):
    B, S = positions.shape
    ts = min(S, PROJ_TILE)
    half = MLA_ROPE // 2
    inv = ROPE_THETA ** (-jnp.arange(0, MLA_ROPE, 2, dtype=F32) / MLA_ROPE)
    inv_full = jnp.zeros((1, LANES), F32)
    inv_full = inv_full.at[0, KR_LANE:KR_LANE + half].set(inv)
    inv_full = inv_full.at[0, KR_LANE + half:KR_LANE + MLA_ROPE].set(inv)
    tab = jax.ShapeDtypeStruct((B, S, LANES), F32)
    spec = pl.BlockSpec((1, ts, LANES), lambda b, t: (b, t, 0))
    return pl.pallas_call(
        _rope_kernel,
        grid=(B, S // ts),
        in_specs=[pl.BlockSpec((1, ts, 1), lambda b, t: (b, t, 0)),
                  pl.BlockSpec((1, LANES), lambda b, t: (0, 0))],
        out_specs=[spec, spec, spec, spec],
        out_shape=[tab, tab, tab, tab],
        compiler_params=pltpu.CompilerParams(
            dimension_semantics=("arbitrary", "arbitrary"), vmem_limit_bytes=VMEM_LIMIT),
        name="rope_tables",
    )(positions.reshape(B, S, 1), inv_full)


def _inproj_kernel(x_ref, g_ref, sc_ref, sh_ref, w_ref, o_ref):
    h = _rms(x_ref[0], g_ref[...]) * (1.0 + sc_ref[0]) + sh_ref[0]
    o_ref[0] = _dot(h.astype(BF16), w_ref[...])


def _inproj_call(x, g, scale, shift, w):
    B, S, D = x.shape
    N = w.shape[1]
    tm = min(S, PROJ_TILE)
    return pl.pallas_call(
        _inproj_kernel,
        grid=(B, S // tm),
        in_specs=[
            pl.BlockSpec((1, tm, D), lambda b, t: (b, t, 0)),
            pl.BlockSpec((1, D), lambda b, t: (0, 0)),
            pl.BlockSpec((1, 1, D), lambda b, t: (b, 0, 0)),
            pl.BlockSpec((1, 1, D), lambda b, t: (b, 0, 0)),
            pl.BlockSpec((D, N), lambda b, t: (0, 0)),
        ],
        out_specs=pl.BlockSpec((1, tm, N), lambda b, t: (b, t, 0)),
        out_shape=jax.ShapeDtypeStruct((B, S, N), F32),
        compiler_params=pltpu.CompilerParams(
            dimension_semantics=("arbitrary", "arbitrary"), vmem_limit_bytes=VMEM_LIMIT),
        name="inproj",
    )(x, g.reshape(1, D), scale.reshape(B, 1, D), shift.reshape(B, 1, D), w)


def _rwkv_kernel(*refs, has_vmix, ts):
    it = iter(refs)
    p_ref = next(it)
    gr_ref = next(it)
    if has_vmix:
        yb_ref = next(it)
        vf_ref = next(it)
    mu_ref = next(it)
    vecs_ref = next(it)
    wdec_ref = next(it)
    wicl_ref = next(it)
    if has_vmix:
        muy_ref = next(it)
        wvm_ref = next(it)
    tri_ref = next(it)
    bd_ref = next(it)
    out_ref = next(it)
    if not has_vmix:
        vf_out_ref = next(it)
    carry_p = next(it)
    carry_y = next(it)
    state = next(it)
    rh_s, pc_s, y_s, bonus_s, ops_s, g_s, h_s, qt_s = (next(it) for _ in range(8))

    t = pl.program_id(1)

    @pl.when(t == 0)
    def _():
        carry_p[...] = jnp.zeros_like(carry_p)
        carry_y[...] = jnp.zeros_like(carry_y)
        state[...] = jnp.zeros_like(state)

    w0 = vecs_ref[0:1, :]
    a0 = vecs_ref[1:2, :]
    k_k = vecs_ref[2:3, :]
    k_a = vecs_ref[3:4, :]
    r_k = vecs_ref[4:5, :]
    lnw = vecs_ref[5:6, :]
    lnb = vecs_ref[6:7, :]
    bd = bd_ref[...]

    def seg(x):
        xb = x.astype(BF16)
        return jnp.concatenate([_dot(xb[:, q * MXU_DIM:(q + 1) * MXU_DIM], bd)
                                for q in range(RWKV_WIDTH // MXU_DIM)], axis=1)

    bf = lambda x_: x_.astype(BF16)
    grp = min(RWKV_GROUP, ts)
    row = lax.broadcasted_iota(jnp.int32, (grp, 1), 0)

    def prepare(g):
        rs = slice(g * grp, (g + 1) * grp)

        def shifted(lo, hi):
            pc_ = p_ref[0, rs, lo:hi]
            prev = jnp.where(row == 0, carry_p[:, lo:hi], pltpu.roll(pc_, 1, 0))
            carry_p[:, lo:hi] = pc_[grp - 1:grp, :]
            return pc_ + (prev - pc_) * mu_ref[:, lo:hi]

        r = shifted(0, RWKV_WIDTH)
        yield
        k = shifted(RWKV_WIDTH, 2 * RWKV_WIDTH)
        yield
        v = shifted(2 * RWKV_WIDTH, 3 * RWKV_WIDTH)
        xl = shifted(3 * RWKV_WIDTH, SHIFT_WIDTH)
        yield
        dec = w0 + _dot(jnp.tanh(xl).astype(BF16), wdec_ref[...])
        lw = -DECAY_SCALE * _sigmoid(dec)
        a = _sigmoid(a0 + _dot(xl.astype(BF16), wicl_ref[...]))
        yield
        if has_vmix:
            yb = yb_ref[0, rs, :]
            prevy = jnp.where(row == 0, carry_y[...], pltpu.roll(yb, 1, 0))
            carry_y[...] = yb[grp - 1:grp, :]
            ys = yb + (prevy - yb) * muy_ref[...]
            v0 = vecs_ref[7:8, :]
            mix = _sigmoid(v0 + _dot(ys.astype(BF16), wvm_ref[...]))
            v = v + (vf_ref[0, rs, :] - v) * mix
        else:
            vf_out_ref[0, rs, :] = v
        yield
        kk = k * k_k
        kk = kk * lax.rsqrt(jnp.maximum(seg(kk * kk), 1e-24))
        yield
        k2 = k * (1.0 + (a - 1.0) * k_a)
        bonus_s[rs, :] = seg(r * k2 * r_k) * v
        yield
        lw_hi, lw_lo = _split(lw)
        tri = tri_ref[...]
        cl = _dot(tri, lw_hi) + _dot(tri, lw_lo)
        ce = jnp.concatenate(
            [jnp.broadcast_to(cl[(c + 1) * CHUNK - 1:(c + 1) * CHUNK, :], (CHUNK, RWKV_WIDTH))
             for c in range(grp // CHUNK)], axis=0)
        yield
        e_out = jnp.exp(-cl)
        pce = jnp.exp(ce)
        rh = r * jnp.exp(cl)
        rh_s[rs, :] = rh
        pc_s[rs, :] = pce
        ops_s[1, rs, :] = bf(rh)
        yield
        kh = k2 * e_out
        ops_s[3, rs, :] = bf(kh)
        ops_s[5, rs, :] = bf(kh * pce)
        yield
        bh = kk * a * e_out
        ops_s[2, rs, :] = bf(bh)
        ops_s[6, rs, :] = bf(bh * pce)
        yield
        ops_s[0, rs, :] = bf(kk * jnp.exp(cl - lw))
        ops_s[4, rs, :] = bf(v)

    ri = lax.broadcasted_iota(jnp.int32, (CHUNK, MXU_DIM), 0)
    lane_head = lax.broadcasted_iota(jnp.int32, (CHUNK, MXU_DIM), 1) // HEAD
    ci = lax.broadcasted_iota(jnp.int32, (CHUNK, MXU_DIM), 1) & (HEAD - 1)
    strict = ci < ri
    incl = ci <= ri
    eye_sb = jnp.where(ci == ri, 1.0, 0.0).astype(F32)
    er = lax.broadcasted_iota(jnp.int32, (MXU_DIM, MXU_DIM), 0)
    ec = lax.broadcasted_iota(jnp.int32, (MXU_DIM, MXU_DIM), 1)
    eye_bd = er == ec
    bdm = (er // HEAD) == (ec // HEAD)

    def expand(xb):
        return jnp.concatenate([xb] * QUAD, axis=0) * bd

    nchunk = ts // CHUNK
    nquad = RWKV_WIDTH // MXU_DIM
    rows = lambda c: slice(c * CHUNK, (c + 1) * CHUNK)
    lanes = lambda q: slice(q * MXU_DIM, (q + 1) * MXU_DIM)
    each = lambda f, *ls: [f(*xs) for xs in zip(*ls)]

    def transitions(g, tick):
        chains = [(c, q) for c in range(g * grp // CHUNK, (g + 1) * grp // CHUNK) for q in range(nquad)]
        op = lambda j: [ops_s[j, rows(c), lanes(q)] for c, q in chains]
        Ab, Rb, Bhb, Khb, Vb, Kpb, Bpb = (op(j) for j in range(7))
        AR = each(lambda a_, r_: jnp.concatenate([a_, r_], axis=0), Ab, Rb)
        sb = each(lambda x_, b_: _dot(x_, expand(b_), NT), AR, Bhb)
        tick()
        sk = each(lambda x_, k_: _dot(x_, expand(k_), NT), AR, Khb)
        tick()
        lab = [jnp.where(strict, x_[:CHUNK], 0.0) for x_ in sb]
        mrb = [bf(jnp.where(incl, x_[CHUNK:], 0.0)) for x_ in sb]
        lak = [bf(jnp.where(strict, x_[:CHUNK], 0.0)) for x_ in sk]
        mrk = [bf(jnp.where(incl, x_[CHUNK:], 0.0)) for x_ in sk]
        T = [eye_sb - l_ for l_ in lab]
        P = each(lambda l_: _dot(bf(l_), expand(bf(l_))), lab)
        tick()
        n = 4
        while n < CHUNK:
            tp = each(lambda t_, p_: _dot(jnp.concatenate([bf(t_), bf(p_)], axis=0), expand(bf(p_))), T, P)
            T = each(lambda t_, x_: t_ + x_[:CHUNK], T, tp)
            P = [x_[CHUNK:] for x_ in tp]
            tick()
            n *= 2
        Tb = each(lambda t_, p_: bf(t_ + _dot(bf(t_), expand(bf(p_)))), T, P)
        tick()
        lm = each(lambda l_, m_, v_: _dot(jnp.concatenate([l_, m_], axis=0), expand(v_)), lak, mrk, Vb)
        lakv = [bf(x_[:CHUNK]) for x_ in lm]
        y0 = [x_[CHUNK:] for x_ in lm]
        tick()
        Wb = each(lambda t_, a_: bf(_dot(t_, expand(a_))), Tb, Ab)
        tick()
        U0b = each(lambda t_, l_: bf(_dot(t_, expand(l_))), Tb, lakv)
        tick()
        mw = each(lambda m_, w_, u_: _dot(m_, jnp.concatenate([expand(w_), expand(u_)], axis=1)),
                  mrb, Wb, U0b)
        tick()
        wtb = each(lambda w_, b_: _dot(w_, b_, TN), Wb, Bpb)
        tick()
        hbd = each(lambda v_, u_, k_, b_: _dot(jnp.concatenate([v_, -u_], axis=0),
                                               jnp.concatenate([k_, b_], axis=0), TN),
                   Vb, U0b, Kpb, Bpb)
        tick()
        for i, (c, q) in enumerate(chains):
            qt_s[c, q] = bf(rh_s[rows(c), lanes(q)] - mw[i][:, :MXU_DIM])
            y_s[rows(c), lanes(q)] = y0[i] - mw[i][:, MXU_DIM:]
            pc = pc_s[c * CHUNK:c * CHUNK + 1, lanes(q)]
            diag = jnp.where(eye_bd, jnp.broadcast_to(pc, (MXU_DIM, MXU_DIM)), 0.0)
            g_s[c, q] = bf(diag - jnp.where(bdm, wtb[i], 0.0))
            hm = hbd[i][0:HEAD]
            for h in range(1, QUAD):
                hm = jnp.where(lane_head == h, hbd[i][h * HEAD:(h + 1) * HEAD], hm)
            h_s[c, q] = hm

    S = [state[q] for q in range(nquad)]

    def finish(g):
        for c in range(g * grp // CHUNK, (g + 1) * grp // CHUNK):
            for q in range(nquad):
                Sb = bf(S[q])
                y_s[rows(c), lanes(q)] = y_s[rows(c), lanes(q)] + _dot(qt_s[c, q], expand(Sb), NT)
                S[q] = _dot(Sb, g_s[c, q]) + h_s[c, q]
            yield
        rs = slice(g * grp, (g + 1) * grp)
        y = y_s[rs, :]
        mean = seg(y) * (1.0 / HEAD)
        d = y - mean
        yield
        var = seg(d * d) * (1.0 / HEAD)
        yn = d * lax.rsqrt(var + GN_EPS) * lnw + lnb
        yield
        out_ref[0, rs, :] = ((yn + bonus_s[rs, :]) * _silu(gr_ref[0, rs, :])).astype(BF16)

    ngroup = ts // grp
    for _ in prepare(0):
        pass
    for g in range(ngroup):
        fill = []
        if g + 1 < ngroup:
            fill.append(prepare(g + 1))
        if g >= 1:
            fill.append(finish(g - 1))
        filler = (None for gen in fill for _ in gen)
        transitions(g, lambda: next(filler, None))
        for _ in filler:
            pass
    for _ in finish(ngroup - 1):
        pass
    for q in range(nquad):
        state[q] = S[q]


def _rwkv_call(proj, v_first, lw):
    B, S, _ = proj.shape
    has_vmix = v_first is not None
    ts = min(S, RWKV_TILE)
    nP = SHIFT_WIDTH
    tok = lambda w, j: pl.BlockSpec((1, ts, w), lambda b, t: (b, t, j))
    full = lambda a: pl.BlockSpec(a.shape, lambda b, t: (0,) * a.ndim)

    args = [proj, proj]
    specs = [tok(nP, C_P // nP), tok(RWKV_WIDTH, C_GR // RWKV_WIDTH)]
    if has_vmix:
        args += [proj, v_first]
        specs += [tok(LANES, C_Y // LANES), tok(RWKV_WIDTH, 0)]
    consts = [lw["mu"], lw["vecs"], lw["wdec"], lw["wicl"]]
    if has_vmix:
        consts += [lw["muy"], lw["wvm"]]
    tid = jnp.arange(min(RWKV_GROUP, ts))
    same_chunk = (tid[:, None] // CHUNK) == (tid[None, :] // CHUNK)
    tri = (same_chunk & (tid[:, None] >= tid[None, :])).astype(BF16)
    consts += [tri, lw["bd"]]
    args += consts
    specs += [full(a) for a in consts]

    out_shape = [jax.ShapeDtypeStruct((B, S, RWKV_WIDTH), BF16)]
    out_specs = [tok(RWKV_WIDTH, 0)]
    if not has_vmix:
        out_shape.append(jax.ShapeDtypeStruct((B, S, RWKV_WIDTH), F32))
        out_specs.append(tok(RWKV_WIDTH, 0))

    big = pltpu.VMEM((ts, RWKV_WIDTH), F32)
    scratch = [pltpu.VMEM((1, nP), F32), pltpu.VMEM((1, LANES), F32),
               pltpu.VMEM((RWKV_WIDTH // MXU_DIM, HEAD, MXU_DIM), F32)] + [big] * 4
    nquad = RWKV_WIDTH // MXU_DIM
    scratch += [pltpu.VMEM((7, ts, RWKV_WIDTH), BF16),
                pltpu.VMEM((ts // CHUNK, nquad, MXU_DIM, MXU_DIM), BF16),
                pltpu.VMEM((ts // CHUNK, nquad, HEAD, MXU_DIM), F32),
                pltpu.VMEM((ts // CHUNK, nquad, CHUNK, MXU_DIM), BF16)]
    outs = pl.pallas_call(
        functools.partial(_rwkv_kernel, has_vmix=has_vmix, ts=ts),
        grid=(B, S // ts),
        in_specs=specs,
        out_specs=out_specs,
        out_shape=out_shape,
        scratch_shapes=scratch,
        compiler_params=pltpu.CompilerParams(
            dimension_semantics=("arbitrary", "arbitrary"), vmem_limit_bytes=VMEM_LIMIT),
        name="rwkv_vmix" if has_vmix else "rwkv_first",
    )(*args)
    if has_vmix:
        return outs[0], v_first
    return outs[0], outs[1]


def _rot_half(x, sina, sinb):
    half = MLA_ROPE // 2
    return pltpu.roll(x, LANES - half, 1) * sina + pltpu.roll(x, half, 1) * sinb


def _mla_prep_kernel(cq_ref, ckv_ref, yb_ref, cosq_ref, cosk_ref, sina_ref, sinb_ref,
                     gq_ref, gkv_ref, wq_ref, wkv_ref, q_ref, k_ref, v_ref, *, scale):
    cosq, cosk, sina, sinb = cosq_ref[0], cosk_ref[0], sina_ref[0], sinb_ref[0]
    cqn = _rms(cq_ref[0], gq_ref[...]).astype(BF16)
    qall = _dot(cqn, wq_ref[...])
    ckvn = _rms(ckv_ref[0], gkv_ref[...]).astype(BF16)
    kvall = _dot(ckvn, wkv_ref[...])
    yb = yb_ref[0]
    kr = yb * cosk + _rot_half(yb, sina, sinb)
    lane = lax.broadcasted_iota(jnp.int32, (1, LANES), 1)
    for h in range(HEADS):
        qh = qall[:, h * LANES:(h + 1) * LANES]
        q_ref[0, h] = ((qh * cosq + _rot_half(qh, sina, sinb)) * scale).astype(BF16)
        k_ref[0, h] = (kvall[:, h * LANES:(h + 1) * LANES] + kr).astype(BF16)
        ones_lane = jnp.where(lane == (HEAD if h % 2 == 0 else 0), 1.0, 0.0)
        v_ref[0, h] = (kvall[:, (HEADS + h) * LANES:(HEADS + h + 1) * LANES] + ones_lane).astype(BF16)


def _mla_prep_call(proj, tabs, gq, gkv, wq, wkv):
    B, S, _ = proj.shape
    tm = min(S, PROJ_TILE)
    tok = lambda w, j: pl.BlockSpec((1, tm, w), lambda b, t: (b, t, j))
    full = lambda a: pl.BlockSpec(a.shape, lambda b, t: (0,) * a.ndim)
    slab =jax.ShapeDtypeStruct((B, HEADS, S, LANES), BF16)
    slab_spec = pl.BlockSpec((1, HEADS, tm, LANES), lambda b, t: (b, 0, t, 0))
    scale = float(HEAD + MLA_ROPE) ** -0.5 * 1.4426950408889634
    return pl.pallas_call(
        functools.partial(_mla_prep_kernel, scale=scale),
        grid=(B, S // tm),
        in_specs=[tok(Q_LORA, C_Q // Q_LORA), tok(KV_LORA, C_KV // KV_LORA), tok(LANES, C_Y // LANES),
                  tok(LANES, 0), tok(LANES, 0), tok(LANES, 0), tok(LANES, 0),
                  full(gq), full(gkv), full(wq), full(wkv)],
        out_specs=[slab_spec, slab_spec, slab_spec],
        out_shape=[slab, slab, slab],
        compiler_params=pltpu.CompilerParams(
            dimension_semantics=("arbitrary", "arbitrary"), vmem_limit_bytes=VMEM_LIMIT),
        name="mla_prep",
    )(proj, proj, proj, *tabs, gq, gkv, wq, wkv)


ATT_HEADS = 8
ATT_BLOCK = 512
ATT_BATCH = 2


def _attn_kernel(qi_ref, ki_ref, q_ref, k_ref, v_ref, g_ref, o_ref, m_s, acc_s, *, blk, nb):
    step = pl.program_id(2)
    qi = qi_ref[step]
    ki = ki_ref[step]
    units = [(b_, h_) for b_ in range(nb) for h_ in range(ATT_HEADS)]

    @pl.when(ki == 0)
    def _():
        m_s[...] = jnp.full_like(m_s, -jnp.inf)
        acc_s[...] = jnp.zeros_like(acc_s)

    def update(diagonal):
        if diagonal:
            kidx = lax.broadcasted_iota(jnp.int32, (blk, blk), 0)
            qidx = lax.broadcasted_iota(jnp.int32, (blk, blk), 1)
            dead = kidx > qidx
        score = lambda u_: _dot(k_ref[u_[0], u_[1]], q_ref[u_[0], u_[1]], NT)

        def accumulate(i_, u_, pt_, alpha_):
            acc_s[i_] = alpha_ * acc_s[i_] + _dot(v_ref[u_[0], u_[1]], pt_, TN)

        st_next = score(units[0])
        pending = None
        for i, u in enumerate(units):
            st = st_next
            if i + 1 < len(units):
                st_next = score(units[i + 1])
            if pending is not None:
                accumulate(*pending)
            if diagonal:
                st = jnp.where(dead, -jnp.inf, st)
            m_prev = m_s[i]
            m_new = jnp.maximum(m_prev, jnp.max(st, axis=0, keepdims=True))
            pt = jnp.exp2(st - m_new).astype(BF16)
            alpha = jnp.exp2(m_prev - m_new)
            m_s[i] = m_new
            pending = (i, u, pt, alpha)
        accumulate(*pending)

    @pl.when(ki < qi)
    def _():
        update(False)

    @pl.when(ki == qi)
    def _():
        update(True)
        upper = lax.broadcasted_iota(jnp.int32, (LANES, blk), 0) < HEAD
        for b_ in range(nb):
            g = _silu(g_ref[b_])
            for j in range(ATT_HEADS // 2):
                even, odd = acc_s[b_ * ATT_HEADS + 2 * j], acc_s[b_ * ATT_HEADS + 2 * j + 1]
                ot = jnp.where(upper, even / even[HEAD:HEAD + 1, :], odd / odd[0:1, :])
                o_ref[b_, :, j * LANES:(j + 1) * LANES] = (
                    ot.T * g[:, j * LANES:(j + 1) * LANES]).astype(BF16)


def _attn_call(q, k, v, proj):
    B, H, S, _ = q.shape
    blk = min(S, ATT_BLOCK)
    nb = ATT_BATCH if B % ATT_BATCH == 0 else 1
    n = S // blk
    pairs = [(i, j) for i in range(n) for j in range(i + 1)]
    qi_tab = jnp.array([i for i, _ in pairs], jnp.int32)
    ki_tab = jnp.array([j for _, j in pairs], jnp.int32)
    wout = ATT_HEADS * HEAD
    gm0 = C_GM // wout
    qspec = pl.BlockSpec((nb, ATT_HEADS, blk, LANES), lambda b, g, s, qt, kt: (b, g, qt[s], 0))
    kspec = pl.BlockSpec((nb, ATT_HEADS, blk, LANES), lambda b, g, s, qt, kt: (b, g, kt[s], 0))
    grid_spec = pltpu.PrefetchScalarGridSpec(
        num_scalar_prefetch=2,
        grid=(B // nb, H // ATT_HEADS, len(pairs)),
        in_specs=[qspec, kspec, kspec,
                  pl.BlockSpec((nb, blk, wout), lambda b, g, s, qt, kt: (b, qt[s], gm0 + g))],
        out_specs=pl.BlockSpec((nb, blk, wout), lambda b, g, s, qt, kt: (b, qt[s], g)),
        scratch_shapes=[pltpu.VMEM((nb * ATT_HEADS, 1, blk), F32),
                        pltpu.VMEM((nb * ATT_HEADS, LANES, blk), F32)],
    )
    return pl.pallas_call(
        functools.partial(_attn_kernel, blk=blk, nb=nb),
        grid_spec=grid_spec,
        out_shape=jax.ShapeDtypeStruct((B, S, H * HEAD), BF16),
        compiler_params=pltpu.CompilerParams(
            dimension_semantics=("arbitrary",) * 3, vmem_limit_bytes=VMEM_LIMIT),
        name="mla_attn",
    )(qi_tab, ki_tab, q, k, v, proj)


def _outproj_kernel(yr_ref, ym_ref, x_ref, gate_ref, w_ref, fg_ref, o_ref, *, final):
    y = _dot(yr_ref[0], w_ref[0:RWKV_WIDTH, :]) + _dot(ym_ref[0], w_ref[RWKV_WIDTH:, :])
    xn = x_ref[0] + gate_ref[0] * y
    if final:
        xn = _rms(xn, fg_ref[...])
    o_ref[0] = xn


def _outproj_call(yr, ym, x, gate, w, fg, final):
    B, S, D = x.shape
    tm = min(S, PROJ_TILE)
    tok = lambda w_: pl.BlockSpec((1, tm, w_), lambda b, t: (b, t, 0))
    return pl.pallas_call(
        functools.partial(_outproj_kernel, final=final),
        grid=(B, S // tm),
        in_specs=[tok(RWKV_WIDTH), tok(RWKV_WIDTH), tok(D),
                  pl.BlockSpec((1, 1, D), lambda b, t: (b, 0, 0)),
                  pl.BlockSpec(w.shape, lambda b, t: (0, 0)),
                  pl.BlockSpec((1, D), lambda b, t: (0, 0))],
        out_specs=tok(D),
        out_shape=jax.ShapeDtypeStruct((B, S, D), F32),
        compiler_params=pltpu.CompilerParams(
            dimension_semantics=("arbitrary", "arbitrary"), vmem_limit_bytes=VMEM_LIMIT),
        name="outproj_final" if final else "outproj",
    )(yr, ym, x, gate.reshape(B, 1, D), w, fg.reshape(1, D))


def _pack_w_in(w_in_l, w_vmix_down_l):
    D = w_in_l.shape[0]
    z = lambda n: jnp.zeros((D, n), F32)
    o_kr = SHIFT_WIDTH + RWKV_WIDTH + Q_LORA + KV_LORA
    o_cq = SHIFT_WIDTH + RWKV_WIDTH
    o_ckv = o_cq + Q_LORA
    o_gm = o_kr + MLA_ROPE
    vm = z(VMIX_LORA) if w_vmix_down_l is None else w_vmix_down_l
    yblk = jnp.concatenate([vm, z(KR_LANE - VMIX_LORA), w_in_l[:, o_kr:o_gm],
                            z(LANES - KR_LANE - MLA_ROPE)], axis=1)
    cols = [w_in_l[:, :SHIFT_WIDTH], yblk, w_in_l[:, o_ckv:o_kr],
            w_in_l[:, SHIFT_WIDTH:o_cq], w_in_l[:, o_gm:IN_WIDTH], w_in_l[:, o_cq:o_ckv]]
    return jnp.concatenate(cols, axis=1).astype(BF16)


def _pack_wq(w_uq_l):
    w = w_uq_l.reshape(Q_LORA, HEADS, HEAD + MLA_ROPE)
    w = jnp.pad(w, ((0, 0), (0, 0), (0, LANES - HEAD - MLA_ROPE)))
    return w.reshape(Q_LORA, HEADS * LANES).astype(BF16)


def _pack_wkv(w_ukv_l):
    w = w_ukv_l.reshape(KV_LORA, HEADS, 2 * HEAD)
    wk = jnp.pad(w[:, :, :HEAD], ((0, 0), (0, 0), (0, LANES - HEAD)))
    wv = w[:, :, HEAD:]
    zero = jnp.zeros_like(wv)
    even = jnp.concatenate([wv, zero], axis=-1)
    odd = jnp.concatenate([zero, wv], axis=-1)
    is_odd = (jnp.arange(HEADS) % 2 == 1)[None, :, None]
    wv2 = jnp.where(is_odd, odd, even)
    return jnp.concatenate([wk.reshape(KV_LORA, -1), wv2.reshape(KV_LORA, -1)], axis=1).astype(BF16)


def _pad_rows(w, lo, total):
    return jnp.pad(w, ((lo, total - lo - w.shape[0]), (0, 0)))


def kernel(x, c, positions, norm_g, w_ada, b_ada, w_in, w_vmix_down, mu_shift, mu_vmix, w0, w_decay_up, a0, w_iclr_up, v0, w_vmix_up, k_k, k_a, r_k, lnx_w, lnx_b, q_norm_g, kv_norm_g, w_uq, w_ukv, w_out, final_g):
    B, S, D = x.shape
    L = w_in.shape[0]
    assert D == D_MODEL and S % CHUNK == 0

    mod = _mod_call(c, w_ada, b_ada)
    tabs = _rope_call(positions)

    qid = jnp.arange(MXU_DIM) // HEAD
    bd = (qid[:, None] == qid[None, :]).astype(BF16)

    v_first = None
    for l in range(L):
        shift, scale, gate = mod[l, :, :D], mod[l, :, D:2 * D], mod[l, :, 2 * D:]
        w_in_p = _pack_w_in(w_in[l], None if l == 0 else w_vmix_down[l - 1])
        proj = _inproj_call(x, norm_g[l], scale, shift, w_in_p)

        vecs = jnp.stack([w0[l], a0[l], k_k[l], k_a[l], r_k[l].reshape(-1), lnx_w[l], lnx_b[l],
                          v0[l - 1] if l > 0 else jnp.zeros((RWKV_WIDTH,), F32)])
        lw = {
            "mu": mu_shift[l].reshape(1, SHIFT_WIDTH),
            "vecs": vecs,
            "wdec": _pad_rows(w_decay_up[l], 0, LANES).astype(BF16),
            "wicl": _pad_rows(w_iclr_up[l], DECAY_LORA, LANES).astype(BF16),
            "bd": bd,
        }
        if l > 0:
            lw["muy"] = jnp.pad(mu_vmix[l - 1], (0, LANES - VMIX_LORA)).reshape(1, LANES)
            lw["wvm"] = _pad_rows(w_vmix_up[l - 1], 0, LANES).astype(BF16)
        y_rwkv, v_first = _rwkv_call(proj, v_first, lw)

        q, k, v = _mla_prep_call(proj, tabs, q_norm_g[l].reshape(1, Q_LORA),
                                 kv_norm_g[l].reshape(1, KV_LORA), _pack_wq(w_uq[l]), _pack_wkv(w_ukv[l]))
        y_mla = _attn_call(q, k, v, proj)

        x = _outproj_call(y_rwkv, y_mla, x, gate, w_out[l].astype(BF16), final_g, final=(l == L - 1))
    return x
```

```python
import functools

import jax
import jax.numpy as jnp
from jax import lax
from jax.experimental import pallas as pl
from jax.experimental.pallas import tpu as pltpu

F32 = jnp.float32
BF16 = jnp.bfloat16

D_MODEL = 1024
RWKV_WIDTH = 512
HEAD = 64
HEADS = 8
DECAY_LORA = 64
ICLR_LORA = 64
VMIX_LORA = 32
MLA_ROPE = 32
Q_LORA = 384
KV_LORA = 256
ROPE_THETA = 10000.0
NORM_EPS = 1e-6
GN_EPS = 64e-5
SHIFT_WIDTH = 3 * RWKV_WIDTH + DECAY_LORA + ICLR_LORA
IN_WIDTH = 3360

LANES = 128
MXU_DIM = 256
QUAD = MXU_DIM // HEAD
VMEM_LIMIT = 48 * 1024 * 1024

C_P = 0
C_Y = SHIFT_WIDTH
C_KV = C_Y + LANES
C_GR = C_KV + KV_LORA
C_GM = C_GR + RWKV_WIDTH
C_Q = C_GM + RWKV_WIDTH
P_WIDTH = C_Q + Q_LORA
KR_LANE = 64

CHUNK = 64
DECAY_SCALE = 0.6065306597126334

PROJ_TILE = 512
RWKV_TILE = 512
RWKV_GROUP = 256

NN = ((1,), (0,))
NT = ((1,), (1,))
TN = ((0,), (0,))


def _dot(a, b, dims=NN):
    return lax.dot_general(a, b, (dims, ((), ())), preferred_element_type=F32)


def _split(x):
    hi = x.astype(BF16)
    lo = (x - hi.astype(F32)).astype(BF16)
    return hi, lo


def _mm(a, b, dims=NN):
    return _dot(a[0], b[0], dims) + _dot(a[0], b[1], dims) + _dot(a[1], b[0], dims)


def _rms(x, g):
    return x * lax.rsqrt(jnp.mean(x * x, axis=-1, keepdims=True) + NORM_EPS) * g


def _sigmoid(x):
    return 0.5 * jnp.tanh(0.5 * x) + 0.5


def _silu(x):
    return x * _sigmoid(x)


def _mod_kernel(c_ref, w_ref, b_ref, o_ref):
    ca = _silu(c_ref[...])
    o_ref[0] = _mm(_split(ca), _split(w_ref[0])) + b_ref[0]


def _mod_call(c, w_ada, b_ada):
    L, D, D3 = w_ada.shape
    B = c.shape[0]
    nb = D3 // D
    return pl.pallas_call(
        _mod_kernel,
        grid=(L, nb),
        in_specs=[
            pl.BlockSpec((B, D), lambda l, j: (0, 0)),
            pl.BlockSpec((1, D, D), lambda l, j: (l, 0, j)),
            pl.BlockSpec((1, 1, D), lambda l, j: (l, 0, j)),
        ],
        out_specs=pl.BlockSpec((1, B, D), lambda l, j: (l, 0, j)),
        out_shape=jax.ShapeDtypeStruct((L, B, D3), F32),
        compiler_params=pltpu.CompilerParams(
            dimension_semantics=("arbitrary", "arbitrary"), vmem_limit_bytes=VMEM_LIMIT),
        name="adaln_mod",
    )(c, w_ada, b_ada.reshape(L, 1, D3))


def _rope_kernel(pos_ref, inv_ref, cosq_ref, cosk_ref, sina_ref, sinb_ref):
    ang = pos_ref[0].astype(F32) * inv_ref[...]
    cs = jnp.cos(ang)
    sn = jnp.sin(ang)
    lane = lax.broadcasted_iota(jnp.int32, ang.shape, 1)
    half = MLA_ROPE // 2
    in_rope = (lane >= KR_LANE) & (lane < KR_LANE + MLA_ROPE)
    first = (lane >= KR_LANE) & (lane < KR_LANE + half)
    second = (lane >= KR_LANE + half) & (lane < KR_LANE + MLA_ROPE)
    cosk = jnp.where(in_rope, cs, 0.0)
    cosk_ref[0] = cosk
    cosq_ref[0] = jnp.where(lane < KR_LANE, 1.0, cosk)
    sina_ref[0] = jnp.where(first, -sn, 0.0)
    sinb_ref[0] = jnp.where(second, sn, 0.0)


def _rope_call(positions):
    B, S = positions.shape
    ts = min(S, PROJ_TILE)
    half = MLA_ROPE // 2
    inv = ROPE_THETA ** (-jnp.arange(0, MLA_ROPE, 2, dtype=F32) / MLA_ROPE)
    inv_full = jnp.zeros((1, LANES), F32)
    inv_full = inv_full.at[0, KR_LANE:KR_LANE + half].set(inv)
    inv_full = inv_full.at[0, KR_LANE + half:KR_LANE + MLA_ROPE].set(inv)
    tab = jax.ShapeDtypeStruct((B, S, LANES), F32)
    spec = pl.BlockSpec((1, ts, LANES), lambda b, t: (b, t, 0))
    return pl.pallas_call(
        _rope_kernel,
        grid=(B, S // ts),
        in_specs=[pl.BlockSpec((1, ts, 1), lambda b, t: (b, t, 0)),
                  pl.BlockSpec((1, LANES), lambda b, t: (0, 0))],
        out_specs=[spec, spec, spec, spec],
        out_shape=[tab, tab, tab, tab],
        compiler_params=pltpu.CompilerParams(
            dimension_semantics=("arbitrary", "arbitrary"), vmem_limit_bytes=VMEM_LIMIT),
        name="rope_tables",
    )(positions.reshape(B, S, 1), inv_full)


def _inproj_kernel(x_ref, g_ref, sc_ref, sh_ref, w_ref, o_ref):
    h = _rms(x_ref[0], g_ref[...]) * (1.0 + sc_ref[0]) + sh_ref[0]
    o_ref[0] = _dot(h.astype(BF16), w_ref[...])


def _layer_spec(stacked, l):
    nd = stacked.ndim
    return pl.BlockSpec((None,) + stacked.shape[1:], lambda b, t: (l,) + (0,) * (nd - 1))


def _inproj_call(x, g, scale, shift, w_all, l):
    B, S, D = x.shape
    N = w_all.shape[2]
    tm = min(S, PROJ_TILE)
    return pl.pallas_call(
        _inproj_kernel,
        grid=(B, S // tm),
        in_specs=[
            pl.BlockSpec((1, tm, D), lambda b, t: (b, t, 0)),
            pl.BlockSpec((1, D), lambda b, t: (0, 0)),
            pl.BlockSpec((1, 1, D), lambda b, t: (b, 0, 0)),
            pl.BlockSpec((1, 1, D), lambda b, t: (b, 0, 0)),
            _layer_spec(w_all, l),
        ],
        out_specs=pl.BlockSpec((1, tm, N), lambda b, t: (b, t, 0)),
        out_shape=jax.ShapeDtypeStruct((B, S, N), F32),
        compiler_params=pltpu.CompilerParams(
            dimension_semantics=("arbitrary", "arbitrary"), vmem_limit_bytes=VMEM_LIMIT),
        name="inproj",
    )(x, g.reshape(1, D), scale.reshape(B, 1, D), shift.reshape(B, 1, D), w_all)


def _rwkv_kernel(*refs, has_vmix, ts):
    it = iter(refs)
    p_ref = next(it)
    gr_ref = next(it)
    if has_vmix:
        yb_ref = next(it)
        vf_ref = next(it)
    mu_ref = next(it)
    vecs_ref = next(it)
    wdec_ref = next(it)
    wicl_ref = next(it)
    if has_vmix:
        muy_ref = next(it)
        wvm_ref = next(it)
    tri_ref = next(it)
    bd_ref = next(it)
    out_ref = next(it)
    if not has_vmix:
        vf_out_ref = next(it)
    carry_p = next(it)
    carry_y = next(it)
    state = next(it)
    rh_s, pc_s, y_s, bonus_s, ops_s, g_s, h_s, qt_s = (next(it) for _ in range(8))

    t = pl.program_id(1)

    @pl.when(t == 0)
    def _():
        carry_p[...] = jnp.zeros_like(carry_p)
        carry_y[...] = jnp.zeros_like(carry_y)
        state[...] = jnp.zeros_like(state)

    w0 = vecs_ref[0:1, :]
    a0 = vecs_ref[1:2, :]
    k_k = vecs_ref[2:3, :]
    k_a = vecs_ref[3:4, :]
    r_k = vecs_ref[4:5, :]
    lnw = vecs_ref[5:6, :]
    lnb = vecs_ref[6:7, :]
    bd = bd_ref[...]

    def seg(x):
        xb = x.astype(BF16)
        return jnp.concatenate([_dot(xb[:, q * MXU_DIM:(q + 1) * MXU_DIM], bd)
                                for q in range(RWKV_WIDTH // MXU_DIM)], axis=1)

    bf = lambda x_: x_.astype(BF16)
    grp = min(RWKV_GROUP, ts)
    row = lax.broadcasted_iota(jnp.int32, (grp, 1), 0)

    def prepare(g):
        rs = slice(g * grp, (g + 1) * grp)

        def shifted(lo, hi):
            pc_ = p_ref[0, rs, lo:hi]
            prev = jnp.where(row == 0, carry_p[:, lo:hi], pltpu.roll(pc_, 1, 0))
            carry_p[:, lo:hi] = pc_[grp - 1:grp, :]
            return pc_ + (prev - pc_) * mu_ref[:, lo:hi]

        r = shifted(0, RWKV_WIDTH)
        yield
        k = shifted(RWKV_WIDTH, 2 * RWKV_WIDTH)
        yield
        v = shifted(2 * RWKV_WIDTH, 3 * RWKV_WIDTH)
        xl = shifted(3 * RWKV_WIDTH, SHIFT_WIDTH)
        yield
        dec = w0 + _dot(jnp.tanh(xl).astype(BF16), wdec_ref[...])
        lw = -DECAY_SCALE * _sigmoid(dec)
        a = _sigmoid(a0 + _dot(xl.astype(BF16), wicl_ref[...]))
        yield
        if has_vmix:
            yb = yb_ref[0, rs, :]
            prevy = jnp.where(row == 0, carry_y[...], pltpu.roll(yb, 1, 0))
            carry_y[...] = yb[grp - 1:grp, :]
            ys = yb + (prevy - yb) * muy_ref[...]
            v0 = vecs_ref[7:8, :]
            mix = _sigmoid(v0 + _dot(ys.astype(BF16), wvm_ref[...]))
            v = v + (vf_ref[0, rs, :] - v) * mix
        else:
            vf_out_ref[0, rs, :] = v
        yield
        kk = k * k_k
        kk = kk * lax.rsqrt(jnp.maximum(seg(kk * kk), 1e-24))
        yield
        k2 = k * (1.0 + (a - 1.0) * k_a)
        bonus_s[rs, :] = seg(r * k2 * r_k) * v
        yield
        lw_hi, lw_lo = _split(lw)
        tri = tri_ref[...]
        cl = _dot(tri, lw_hi) + _dot(tri, lw_lo)
        ce = jnp.concatenate(
            [jnp.broadcast_to(cl[(c + 1) * CHUNK - 1:(c + 1) * CHUNK, :], (CHUNK, RWKV_WIDTH))
             for c in range(grp // CHUNK)], axis=0)
        yield
        e_out = jnp.exp(-cl)
        pce = jnp.exp(ce)
        rh = r * jnp.exp(cl)
        rh_s[rs, :] = rh
        pc_s[rs, :] = pce
        ops_s[1, rs, :] = bf(rh)
        yield
        kh = k2 * e_out
        ops_s[3, rs, :] = bf(kh)
        ops_s[5, rs, :] = bf(kh * pce)
        yield
        bh = kk * a * e_out
        ops_s[2, rs, :] = bf(bh)
        ops_s[6, rs, :] = bf(bh * pce)
        yield
        ops_s[0, rs, :] = bf(kk * jnp.exp(cl - lw))
        ops_s[4, rs, :] = bf(v)

    ri = lax.broadcasted_iota(jnp.int32, (CHUNK, MXU_DIM), 0)
    lane_head = lax.broadcasted_iota(jnp.int32, (CHUNK, MXU_DIM), 1) // HEAD
    ci = lax.broadcasted_iota(jnp.int32, (CHUNK, MXU_DIM), 1) & (HEAD - 1)
    strict = ci < ri
    incl = ci <= ri
    eye_sb = jnp.where(ci == ri, 1.0, 0.0).astype(F32)
    er = lax.broadcasted_iota(jnp.int32, (MXU_DIM, MXU_DIM), 0)
    ec = lax.broadcasted_iota(jnp.int32, (MXU_DIM, MXU_DIM), 1)
    eye_bd = er == ec
    bdm = (er // HEAD) == (ec // HEAD)

    def expand(xb):
        return jnp.concatenate([xb] * QUAD, axis=0) * bd

    nchunk = ts // CHUNK
    nquad = RWKV_WIDTH // MXU_DIM
    rows = lambda c: slice(c * CHUNK, (c + 1) * CHUNK)
    lanes = lambda q: slice(q * MXU_DIM, (q + 1) * MXU_DIM)
    each = lambda f, *ls: [f(*xs) for xs in zip(*ls)]

    def transitions(g, tick):
        chains = [(c, q) for c in range(g * grp // CHUNK, (g + 1) * grp // CHUNK) for q in range(nquad)]
        op = lambda j: [ops_s[j, rows(c), lanes(q)] for c, q in chains]
        Ab, Rb, Bhb, Khb, Vb, Kpb, Bpb = (op(j) for j in range(7))
        AR = each(lambda a_, r_: jnp.concatenate([a_, r_], axis=0), Ab, Rb)
        sb = each(lambda x_, b_: _dot(x_, expand(b_), NT), AR, Bhb)
        tick()
        sk = each(lambda x_, k_: _dot(x_, expand(k_), NT), AR, Khb)
        tick()
        lab = [jnp.where(strict, x_[:CHUNK], 0.0) for x_ in sb]
        mrb = [bf(jnp.where(incl, x_[CHUNK:], 0.0)) for x_ in sb]
        lak = [bf(jnp.where(strict, x_[:CHUNK], 0.0)) for x_ in sk]
        mrk = [bf(jnp.where(incl, x_[CHUNK:], 0.0)) for x_ in sk]
        T = [eye_sb - l_ for l_ in lab]
        P = each(lambda l_: _dot(bf(l_), expand(bf(l_))), lab)
        tick()
        n = 4
        while n < CHUNK:
            tp = each(lambda t_, p_: _dot(jnp.concatenate([bf(t_), bf(p_)], axis=0), expand(bf(p_))), T, P)
            T = each(lambda t_, x_: t_ + x_[:CHUNK], T, tp)
            P = [x_[CHUNK:] for x_ in tp]
            tick()
            n *= 2
        Tb = each(lambda t_, p_: bf(t_ + _dot(bf(t_), expand(bf(p_)))), T, P)
        tick()
        lm = each(lambda l_, m_, v_: _dot(jnp.concatenate([l_, m_], axis=0), expand(v_)), lak, mrk, Vb)
        lakv = [bf(x_[:CHUNK]) for x_ in lm]
        y0 = [x_[CHUNK:] for x_ in lm]
        tick()
        Wb = each(lambda t_, a_: bf(_dot(t_, expand(a_))), Tb, Ab)
        tick()
        U0b = each(lambda t_, l_: bf(_dot(t_, expand(l_))), Tb, lakv)
        tick()
        mw = each(lambda m_, w_, u_: _dot(m_, jnp.concatenate([expand(w_), expand(u_)], axis=1)),
                  mrb, Wb, U0b)
        tick()
        wtb = each(lambda w_, b_: _dot(w_, b_, TN), Wb, Bpb)
        tick()
        hbd = each(lambda v_, u_, k_, b_: _dot(jnp.concatenate([v_, -u_], axis=0),
                                               jnp.concatenate([k_, b_], axis=0), TN),
                   Vb, U0b, Kpb, Bpb)
        tick()
        for i, (c, q) in enumerate(chains):
            qt_s[c, q] = bf(rh_s[rows(c), lanes(q)] - mw[i][:, :MXU_DIM])
            y_s[rows(c), lanes(q)] = y0[i] - mw[i][:, MXU_DIM:]
            pc = pc_s[c * CHUNK:c * CHUNK + 1, lanes(q)]
            diag = jnp.where(eye_bd, jnp.broadcast_to(pc, (MXU_DIM, MXU_DIM)), 0.0)
            g_s[c, q] = bf(diag - jnp.where(bdm, wtb[i], 0.0))
            hm = hbd[i][0:HEAD]
            for h in range(1, QUAD):
                hm = jnp.where(lane_head == h, hbd[i][h * HEAD:(h + 1) * HEAD], hm)
            h_s[c, q] = hm

    S = [state[q] for q in range(nquad)]

    def finish(g):
        for c in range(g * grp // CHUNK, (g + 1) * grp // CHUNK):
            for q in range(nquad):
                Sb = bf(S[q])
                y_s[rows(c), lanes(q)] = y_s[rows(c), lanes(q)] + _dot(qt_s[c, q], expand(Sb), NT)
                S[q] = _dot(Sb, g_s[c, q]) + h_s[c, q]
            yield
        rs = slice(g * grp, (g + 1) * grp)
        y = y_s[rs, :]
        mean = seg(y) * (1.0 / HEAD)
        d = y - mean
        yield
        var = seg(d * d) * (1.0 / HEAD)
        yn = d * lax.rsqrt(var + GN_EPS) * lnw + lnb
        yield
        out_ref[0, rs, :] = ((yn + bonus_s[rs, :]) * _silu(gr_ref[0, rs, :])).astype(BF16)

    ngroup = ts // grp
    for _ in prepare(0):
        pass
    for g in range(ngroup):
        fill = []
        if g + 1 < ngroup:
            fill.append(prepare(g + 1))
        if g >= 1:
            fill.append(finish(g - 1))
        filler = (None for gen in fill for _ in gen)
        transitions(g, lambda: next(filler, None))
        for _ in filler:
            pass
    for _ in finish(ngroup - 1):
        pass
    for q in range(nquad):
        state[q] = S[q]


def _rwkv_call(proj, v_first, lw):
    B, S, _ = proj.shape
    has_vmix = v_first is not None
    ts = min(S, RWKV_TILE)
    nP = SHIFT_WIDTH
    tok = lambda w, j: pl.BlockSpec((1, ts, w), lambda b, t: (b, t, j))
    full = lambda a: pl.BlockSpec(a.shape, lambda b, t: (0,) * a.ndim)

    args = [proj, proj]
    specs = [tok(nP, C_P // nP), tok(RWKV_WIDTH, C_GR // RWKV_WIDTH)]
    if has_vmix:
        args += [proj, v_first]
        specs += [tok(LANES, C_Y // LANES), tok(RWKV_WIDTH, 0)]
    consts = [lw["mu"], lw["vecs"], lw["wdec"], lw["wicl"]]
    if has_vmix:
        consts += [lw["muy"], lw["wvm"]]
    tid = jnp.arange(min(RWKV_GROUP, ts))
    same_chunk = (tid[:, None] // CHUNK) == (tid[None, :] // CHUNK)
    tri = (same_chunk & (tid[:, None] >= tid[None, :])).astype(BF16)
    consts += [tri, lw["bd"]]
    args += consts
    specs += [full(a) for a in consts]

    out_shape = [jax.ShapeDtypeStruct((B, S, RWKV_WIDTH), BF16)]
    out_specs = [tok(RWKV_WIDTH, 0)]
    if not has_vmix:
        out_shape.append(jax.ShapeDtypeStruct((B, S, RWKV_WIDTH), F32))
        out_specs.append(tok(RWKV_WIDTH, 0))

    big = pltpu.VMEM((ts, RWKV_WIDTH), F32)
    scratch = [pltpu.VMEM((1, nP), F32), pltpu.VMEM((1, LANES), F32),
               pltpu.VMEM((RWKV_WIDTH // MXU_DIM, HEAD, MXU_DIM), F32)] + [big] * 4
    nquad = RWKV_WIDTH // MXU_DIM
    scratch += [pltpu.VMEM((7, ts, RWKV_WIDTH), BF16),
                pltpu.VMEM((ts // CHUNK, nquad, MXU_DIM, MXU_DIM), BF16),
                pltpu.VMEM((ts // CHUNK, nquad, HEAD, MXU_DIM), F32),
                pltpu.VMEM((ts // CHUNK, nquad, CHUNK, MXU_DIM), BF16)]
    outs = pl.pallas_call(
        functools.partial(_rwkv_kernel, has_vmix=has_vmix, ts=ts),
        grid=(B, S // ts),
        in_specs=specs,
        out_specs=out_specs,
        out_shape=out_shape,
        scratch_shapes=scratch,
        compiler_params=pltpu.CompilerParams(
            dimension_semantics=("arbitrary", "arbitrary"), vmem_limit_bytes=VMEM_LIMIT),
        name="rwkv_vmix" if has_vmix else "rwkv_first",
    )(*args)
    if has_vmix:
        return outs[0], v_first
    return outs[0], outs[1]


def _rot_half(x, sina, sinb):
    half = MLA_ROPE // 2
    return pltpu.roll(x, LANES - half, 1) * sina + pltpu.roll(x, half, 1) * sinb


def _mla_prep_kernel(cq_ref, ckv_ref, yb_ref, cosq_ref, cosk_ref, sina_ref, sinb_ref,
                     gq_ref, gkv_ref, wq_ref, wkv_ref, q_ref, k_ref, v_ref, *, scale):
    cosq, cosk, sina, sinb = cosq_ref[0], cosk_ref[0], sina_ref[0], sinb_ref[0]
    cqn = _rms(cq_ref[0], gq_ref[...]).astype(BF16)
    qall = _dot(cqn, wq_ref[...])
    ckvn = _rms(ckv_ref[0], gkv_ref[...]).astype(BF16)
    kvall = _dot(ckvn, wkv_ref[...])
    yb = yb_ref[0]
    kr = yb * cosk + _rot_half(yb, sina, sinb)
    lane = lax.broadcasted_iota(jnp.int32, (1, LANES), 1)
    for h in range(HEADS):
        qh = qall[:, h * LANES:(h + 1) * LANES]
        q_ref[0, h] = ((qh * cosq + _rot_half(qh, sina, sinb)) * scale).astype(BF16)
        k_ref[0, h] = (kvall[:, h * LANES:(h + 1) * LANES] + kr).astype(BF16)
        ones_lane = jnp.where(lane == (HEAD if h % 2 == 0 else 0), 1.0, 0.0)
        v_ref[0, h] = (kvall[:, (HEADS + h) * LANES:(HEADS + h + 1) * LANES] + ones_lane).astype(BF16)


def _mla_prep_call(proj, tabs, gq, gkv, wq_all, wkv_all, l):
    B, S, _ = proj.shape
    tm = min(S, PROJ_TILE)
    tok = lambda w, j: pl.BlockSpec((1, tm, w), lambda b, t: (b, t, j))
    full = lambda a: pl.BlockSpec(a.shape, lambda b, t: (0,) * a.ndim)
    slab =jax.ShapeDtypeStruct((B, HEADS, S, LANES), BF16)
    slab_spec = pl.BlockSpec((1, HEADS, tm, LANES), lambda b, t: (b, 0, t, 0))
    scale = float(HEAD + MLA_ROPE) ** -0.5 * 1.4426950408889634
    return pl.pallas_call(
        functools.partial(_mla_prep_kernel, scale=scale),
        grid=(B, S // tm),
        in_specs=[tok(Q_LORA, C_Q // Q_LORA), tok(KV_LORA, C_KV // KV_LORA), tok(LANES, C_Y // LANES),
                  tok(LANES, 0), tok(LANES, 0), tok(LANES, 0), tok(LANES, 0),
                  full(gq), full(gkv), _layer_spec(wq_all, l), _layer_spec(wkv_all, l)],
        out_specs=[slab_spec, slab_spec, slab_spec],
        out_shape=[slab, slab, slab],
        compiler_params=pltpu.CompilerParams(
            dimension_semantics=("arbitrary", "arbitrary"), vmem_limit_bytes=VMEM_LIMIT),
        name="mla_prep",
    )(proj, proj, proj, *tabs, gq, gkv, wq_all, wkv_all)


ATT_HEADS = 8
ATT_BLOCK = 512
ATT_BATCH = 2


def _attn_kernel(qi_ref, ki_ref, q_ref, k_ref, v_ref, g_ref, o_ref, m_s, acc_s, *, blk, nb):
    step = pl.program_id(2)
    qi = qi_ref[step]
    ki = ki_ref[step]
    units = [(b_, h_) for b_ in range(nb) for h_ in range(ATT_HEADS)]

    @pl.when(ki == 0)
    def _():
        m_s[...] = jnp.full_like(m_s, -jnp.inf)
        acc_s[...] = jnp.zeros_like(acc_s)

    half = blk // 2

    def update(diagonal):
        if diagonal:
            dead0 = (lax.broadcasted_iota(jnp.int32, (half, blk), 0)
                     > lax.broadcasted_iota(jnp.int32, (half, blk), 1))
            dead1 = (lax.broadcasted_iota(jnp.int32, (half, half), 0)
                     > lax.broadcasted_iota(jnp.int32, (half, half), 1))

        def score(u_):
            b_, h_ = u_
            if not diagonal:
                return (_dot(k_ref[b_, h_], q_ref[b_, h_], NT),)
            return (_dot(k_ref[b_, h_, 0:half, :], q_ref[b_, h_], NT),
                    _dot(k_ref[b_, h_, half:blk, :], q_ref[b_, h_, half:blk, :], NT))

        def accumulate(i_, u_, pts_, alpha_):
            b_, h_ = u_
            if not diagonal:
                acc_s[i_] = alpha_ * acc_s[i_] + _dot(v_ref[b_, h_], pts_[0], TN)
            else:
                pv1 = _dot(v_ref[b_, h_, half:blk, :], pts_[1], TN)
                pv = _dot(v_ref[b_, h_, 0:half, :], pts_[0], TN)
                pv = jnp.concatenate([pv[:, 0:half], pv[:, half:blk] + pv1], axis=1)
                acc_s[i_] = alpha_ * acc_s[i_] + pv

        st_next = score(units[0])
        pending = None
        for i, u in enumerate(units):
            sts = st_next
            if i + 1 < len(units):
                st_next = score(units[i + 1])
            if pending is not None:
                accumulate(*pending)
            m_prev = m_s[i]
            if not diagonal:
                m_new = jnp.maximum(m_prev, jnp.max(sts[0], axis=0, keepdims=True))
                pts = (jnp.exp2(sts[0] - m_new).astype(BF16),)
            else:
                st0 = jnp.where(dead0, -jnp.inf, sts[0])
                st1 = jnp.where(dead1, -jnp.inf, sts[1])
                mx0 = jnp.max(st0, axis=0, keepdims=True)
                mx1 = jnp.max(st1, axis=0, keepdims=True)
                m_cur = jnp.concatenate([mx0[:, 0:half], jnp.maximum(mx0[:, half:blk], mx1)], axis=1)
                m_new = jnp.maximum(m_prev, m_cur)
                pts = (jnp.exp2(st0 - m_new).astype(BF16),
                       jnp.exp2(st1 - m_new[:, half:blk]).astype(BF16))
            alpha = jnp.exp2(m_prev - m_new)
            m_s[i] = m_new
            pending = (i, u, pts, alpha)
        accumulate(*pending)

    @pl.when(ki < qi)
    def _():
        update(False)

    @pl.when(ki == qi)
    def _():
        update(True)
        upper = lax.broadcasted_iota(jnp.int32, (LANES, blk), 0) < HEAD
        for b_ in range(nb):
            g = _silu(g_ref[b_])
            for j in range(ATT_HEADS // 2):
                even, odd = acc_s[b_ * ATT_HEADS + 2 * j], acc_s[b_ * ATT_HEADS + 2 * j + 1]
                ot = jnp.where(upper, even / even[HEAD:HEAD + 1, :], odd / odd[0:1, :])
                o_ref[b_, :, j * LANES:(j + 1) * LANES] = (
                    ot.T * g[:, j * LANES:(j + 1) * LANES]).astype(BF16)


def _attn_call(q, k, v, proj):
    B, H, S, _ = q.shape
    blk = min(S, ATT_BLOCK)
    nb = ATT_BATCH if B % ATT_BATCH == 0 else 1
    n = S // blk
    pairs = [(i, j) for i in range(n) for j in range(i + 1)]
    qi_tab = jnp.array([i for i, _ in pairs], jnp.int32)
    ki_tab = jnp.array([j for _, j in pairs], jnp.int32)
    wout = ATT_HEADS * HEAD
    gm0 = C_GM // wout
    qspec = pl.BlockSpec((nb, ATT_HEADS, blk, LANES), lambda b, g, s, qt, kt: (b, g, qt[s], 0))
    kspec = pl.BlockSpec((nb, ATT_HEADS, blk, LANES), lambda b, g, s, qt, kt: (b, g, kt[s], 0))
    grid_spec = pltpu.PrefetchScalarGridSpec(
        num_scalar_prefetch=2,
        grid=(B // nb, H // ATT_HEADS, len(pairs)),
        in_specs=[qspec, kspec, kspec,
                  pl.BlockSpec((nb, blk, wout), lambda b, g, s, qt, kt: (b, qt[s], gm0 + g))],
        out_specs=pl.BlockSpec((nb, blk, wout), lambda b, g, s, qt, kt: (b, qt[s], g)),
        scratch_shapes=[pltpu.VMEM((nb * ATT_HEADS, 1, blk), F32),
                        pltpu.VMEM((nb * ATT_HEADS, LANES, blk), F32)],
    )
    return pl.pallas_call(
        functools.partial(_attn_kernel, blk=blk, nb=nb),
        grid_spec=grid_spec,
        out_shape=jax.ShapeDtypeStruct((B, S, H * HEAD), BF16),
        compiler_params=pltpu.CompilerParams(
            dimension_semantics=("arbitrary",) * 3, vmem_limit_bytes=VMEM_LIMIT),
        name="mla_attn",
    )(qi_tab, ki_tab, q, k, v, proj)


def _outproj_kernel(yr_ref, ym_ref, x_ref, gate_ref, w_ref, fg_ref, o_ref, *, final):
    y = _dot(yr_ref[0], w_ref[0:RWKV_WIDTH, :]) + _dot(ym_ref[0], w_ref[RWKV_WIDTH:, :])
    xn = x_ref[0] + gate_ref[0] * y
    if final:
        xn = _rms(xn, fg_ref[...])
    o_ref[0] = xn


def _outproj_call(yr, ym, x, gate, w_all, l, fg, final):
    B, S, D = x.shape
    tm = min(S, PROJ_TILE)
    tok = lambda w_: pl.BlockSpec((1, tm, w_), lambda b, t: (b, t, 0))
    return pl.pallas_call(
        functools.partial(_outproj_kernel, final=final),
        grid=(B, S // tm),
        in_specs=[tok(RWKV_WIDTH), tok(RWKV_WIDTH), tok(D),
                  pl.BlockSpec((1, 1, D), lambda b, t: (b, 0, 0)),
                  _layer_spec(w_all, l),
                  pl.BlockSpec((1, D), lambda b, t: (0, 0))],
        out_specs=tok(D),
        out_shape=jax.ShapeDtypeStruct((B, S, D), F32),
        compiler_params=pltpu.CompilerParams(
            dimension_semantics=("arbitrary", "arbitrary"), vmem_limit_bytes=VMEM_LIMIT),
        name="outproj_final" if final else "outproj",
    )(yr, ym, x, gate.reshape(B, 1, D), w_all, fg.reshape(1, D))


def _pack_w_in(w_in, w_vmix_down):
    L, D, _ = w_in.shape
    z = lambda n: jnp.zeros((L, D, n), F32)
    o_kr = SHIFT_WIDTH + RWKV_WIDTH + Q_LORA + KV_LORA
    o_cq = SHIFT_WIDTH + RWKV_WIDTH
    o_ckv = o_cq + Q_LORA
    o_gm = o_kr + MLA_ROPE
    vm = jnp.concatenate([jnp.zeros((1, D, VMIX_LORA), F32), w_vmix_down], axis=0)
    yblk = jnp.concatenate([vm, z(KR_LANE - VMIX_LORA), w_in[:, :, o_kr:o_gm],
                            z(LANES - KR_LANE - MLA_ROPE)], axis=2)
    cols = [w_in[:, :, :SHIFT_WIDTH], yblk, w_in[:, :, o_ckv:o_kr],
            w_in[:, :, SHIFT_WIDTH:o_cq], w_in[:, :, o_gm:IN_WIDTH], w_in[:, :, o_cq:o_ckv]]
    return jnp.concatenate(cols, axis=2).astype(BF16)


def _pack_wq(w_uq):
    L = w_uq.shape[0]
    w = w_uq.reshape(L, Q_LORA, HEADS, HEAD + MLA_ROPE)
    w = jnp.pad(w, ((0, 0), (0, 0), (0, 0), (0, LANES - HEAD - MLA_ROPE)))
    return w.reshape(L, Q_LORA, HEADS * LANES).astype(BF16)


def _pack_wkv(w_ukv):
    L = w_ukv.shape[0]
    w = w_ukv.reshape(L, KV_LORA, HEADS, 2 * HEAD)
    wk = jnp.pad(w[..., :HEAD], ((0, 0), (0, 0), (0, 0), (0, LANES - HEAD)))
    wv = w[..., HEAD:]
    zero = jnp.zeros_like(wv)
    even = jnp.concatenate([wv, zero], axis=-1)
    odd = jnp.concatenate([zero, wv], axis=-1)
    is_odd = (jnp.arange(HEADS) % 2 == 1)[None, None, :, None]
    wv2 = jnp.where(is_odd, odd, even)
    return jnp.concatenate([wk.reshape(L, KV_LORA, -1), wv2.reshape(L, KV_LORA, -1)], axis=2).astype(BF16)


def _pad_rows(w, lo, total):
    return jnp.pad(w, ((lo, total - lo - w.shape[0]), (0, 0)))


def kernel(x, c, positions, norm_g, w_ada, b_ada, w_in, w_vmix_down, mu_shift, mu_vmix, w0, w_decay_up, a0, w_iclr_up, v0, w_vmix_up, k_k, k_a, r_k, lnx_w, lnx_b, q_norm_g, kv_norm_g, w_uq, w_ukv, w_out, final_g):
    B, S, D = x.shape
    L = w_in.shape[0]
    assert D == D_MODEL and S % CHUNK == 0

    mod = _mod_call(c, w_ada, b_ada)
    tabs = _rope_call(positions)

    qid = jnp.arange(MXU_DIM) // HEAD
    bd = (qid[:, None] == qid[None, :]).astype(BF16)

    w_in_all = _pack_w_in(w_in, w_vmix_down)
    wq_all, wkv_all = _pack_wq(w_uq), _pack_wkv(w_ukv)
    w_out_all = w_out.astype(BF16)

    v_first = None
    for l in range(L):
        shift, scale, gate = mod[l, :, :D], mod[l, :, D:2 * D], mod[l, :, 2 * D:]
        proj = _inproj_call(x, norm_g[l], scale, shift, w_in_all, l)

        vecs = jnp.stack([w0[l], a0[l], k_k[l], k_a[l], r_k[l].reshape(-1), lnx_w[l], lnx_b[l],
                          v0[l - 1] if l > 0 else jnp.zeros((RWKV_WIDTH,), F32)])
        lw = {
            "mu": mu_shift[l].reshape(1, SHIFT_WIDTH),
            "vecs": vecs,
            "wdec": _pad_rows(w_decay_up[l], 0, LANES).astype(BF16),
            "wicl": _pad_rows(w_iclr_up[l], DECAY_LORA, LANES).astype(BF16),
            "bd": bd,
        }
        if l > 0:
            lw["muy"] = jnp.pad(mu_vmix[l - 1], (0, LANES - VMIX_LORA)).reshape(1, LANES)
            lw["wvm"] = _pad_rows(w_vmix_up[l - 1], 0, LANES).astype(BF16)
        y_rwkv, v_first = _rwkv_call(proj, v_first, lw)

        q, k, v = _mla_prep_call(proj, tabs, q_norm_g[l].reshape(1, Q_LORA),
                                 kv_norm_g[l].reshape(1, KV_LORA), wq_all, wkv_all, l)
        y_mla = _attn_call(q, k, v, proj)

        x = _outproj_call(y_rwkv, y_mla, x, gate, w_out_all, l, final_g, final=(l == L - 1))
    return x
```

```python
import functools

import jax
import jax.numpy as jnp
from jax import lax
from jax.experimental import pallas as pl
from jax.experimental.pallas import tpu as pltpu

F32 = jnp.float32
BF16 = jnp.bfloat16

D_MODEL = 1024
RWKV_WIDTH = 512
HEAD = 64
HEADS = 8
DECAY_LORA = 64
ICLR_LORA = 64
VMIX_LORA = 32
MLA_ROPE = 32
Q_LORA = 384
KV_LORA = 256
ROPE_THETA = 10000.0
NORM_EPS = 1e-6
GN_EPS = 64e-5
SHIFT_WIDTH = 3 * RWKV_WIDTH + DECAY_LORA + ICLR_LORA
IN_WIDTH = 3360

LANES = 128
MXU_DIM = 256
QUAD = MXU_DIM // HEAD
VMEM_LIMIT = 48 * 1024 * 1024

C_P = 0
C_Y = SHIFT_WIDTH
C_KV = C_Y + LANES
C_GR = C_KV + KV_LORA
C_GM = C_GR + RWKV_WIDTH
C_Q = C_GM + RWKV_WIDTH
P_WIDTH = C_Q + Q_LORA
KR_LANE = 64

CHUNK = 64
DECAY_SCALE = 0.6065306597126334

PROJ_TILE = 512
STREAM_TILE = 1024
RWKV_TILE = 512
RWKV_GROUP = 256

NN = ((1,), (0,))
NT = ((1,), (1,))
TN = ((0,), (0,))


def _dot(a, b, dims=NN):
    return lax.dot_general(a, b, (dims, ((), ())), preferred_element_type=F32)


def _split(x):
    hi = x.astype(BF16)
    lo = (x - hi.astype(F32)).astype(BF16)
    return hi, lo


def _mm(a, b, dims=NN):
    return _dot(a[0], b[0], dims) + _dot(a[0], b[1], dims) + _dot(a[1], b[0], dims)


def _rms(x, g):
    return x * lax.rsqrt(jnp.mean(x * x, axis=-1, keepdims=True) + NORM_EPS) * g


def _sigmoid(x):
    return 0.5 * jnp.tanh(0.5 * x) + 0.5


def _silu(x):
    return x * _sigmoid(x)


def _mod_kernel(c_ref, w_ref, b_ref, o_ref):
    ca = _silu(c_ref[...])
    o_ref[0] = _mm(_split(ca), _split(w_ref[0])) + b_ref[0]


def _mod_call(c, w_ada, b_ada):
    L, D, D3 = w_ada.shape
    B = c.shape[0]
    nb = D3 // D
    return pl.pallas_call(
        _mod_kernel,
        grid=(L, nb),
        in_specs=[
            pl.BlockSpec((B, D), lambda l, j: (0, 0)),
            pl.BlockSpec((1, D, D), lambda l, j: (l, 0, j)),
            pl.BlockSpec((1, 1, D), lambda l, j: (l, 0, j)),
        ],
        out_specs=pl.BlockSpec((1, B, D), lambda l, j: (l, 0, j)),
        out_shape=jax.ShapeDtypeStruct((L, B, D3), F32),
        compiler_params=pltpu.CompilerParams(
            dimension_semantics=("arbitrary", "arbitrary"), vmem_limit_bytes=VMEM_LIMIT),
        name="adaln_mod",
    )(c, w_ada, b_ada.reshape(L, 1, D3))


def _rope_kernel(pos_ref, inv_ref, cosq_ref, cosk_ref, sina_ref, sinb_ref):
    ang = pos_ref[0].astype(F32) * inv_ref[...]
    cs = jnp.cos(ang)
    sn = jnp.sin(ang)
    lane = lax.broadcasted_iota(jnp.int32, ang.shape, 1)
    half = MLA_ROPE // 2
    in_rope = (lane >= KR_LANE) & (lane < KR_LANE + MLA_ROPE)
    first = (lane >= KR_LANE) & (lane < KR_LANE + half)
    second = (lane >= KR_LANE + half) & (lane < KR_LANE + MLA_ROPE)
    cosk = jnp.where(in_rope, cs, 0.0)
    cosk_ref[0] = cosk
    cosq_ref[0] = jnp.where(lane < KR_LANE, 1.0, cosk)
    sina_ref[0] = jnp.where(first, -sn, 0.0)
    sinb_ref[0] = jnp.where(second, sn, 0.0)


def _rope_call(positions):
    B, S = positions.shape
    ts = min(S, PROJ_TILE)
    half = MLA_ROPE // 2
    inv = ROPE_THETA ** (-jnp.arange(0, MLA_ROPE, 2, dtype=F32) / MLA_ROPE)
    inv_full = jnp.zeros((1, LANES), F32)
    inv_full = inv_full.at[0, KR_LANE:KR_LANE + half].set(inv)
    inv_full = inv_full.at[0, KR_LANE + half:KR_LANE + MLA_ROPE].set(inv)
    tab = jax.ShapeDtypeStruct((B, S, LANES), F32)
    spec = pl.BlockSpec((1, ts, LANES), lambda b, t: (b, t, 0))
    return pl.pallas_call(
        _rope_kernel,
        grid=(B, S // ts),
        in_specs=[pl.BlockSpec((1, ts, 1), lambda b, t: (b, t, 0)),
                  pl.BlockSpec((1, LANES), lambda b, t: (0, 0))],
        out_specs=[spec, spec, spec, spec],
        out_shape=[tab, tab, tab, tab],
        compiler_params=pltpu.CompilerParams(
            dimension_semantics=("arbitrary", "arbitrary"), vmem_limit_bytes=VMEM_LIMIT),
        name="rope_tables",
    )(positions.reshape(B, S, 1), inv_full)


def _inproj_kernel(x_ref, g_ref, sc_ref, sh_ref, w_ref, o_ref):
    h = _rms(x_ref[0], g_ref[...]) * (1.0 + sc_ref[0]) + sh_ref[0]
    o_ref[0] = _dot(h.astype(BF16), w_ref[...])


def _layer_spec(stacked, l):
    nd = stacked.ndim
    return pl.BlockSpec((None,) + stacked.shape[1:], lambda b, t: (l,) + (0,) * (nd - 1))


def _inproj_call(x, g, scale, shift, w_all, l):
    B, S, D = x.shape
    N = w_all.shape[2]
    tm = min(S, PROJ_TILE)
    return pl.pallas_call(
        _inproj_kernel,
        grid=(B, S // tm),
        in_specs=[
            pl.BlockSpec((1, tm, D), lambda b, t: (b, t, 0)),
            pl.BlockSpec((1, D), lambda b, t: (0, 0)),
            pl.BlockSpec((1, 1, D), lambda b, t: (b, 0, 0)),
            pl.BlockSpec((1, 1, D), lambda b, t: (b, 0, 0)),
            _layer_spec(w_all, l),
        ],
        out_specs=pl.BlockSpec((1, tm, N), lambda b, t: (b, t, 0)),
        out_shape=jax.ShapeDtypeStruct((B, S, N), F32),
        compiler_params=pltpu.CompilerParams(
            dimension_semantics=("arbitrary", "arbitrary"), vmem_limit_bytes=VMEM_LIMIT),
        name="inproj",
    )(x, g.reshape(1, D), scale.reshape(B, 1, D), shift.reshape(B, 1, D), w_all)


def _rwkv_kernel(*refs, has_vmix, ts):
    it = iter(refs)
    p_ref = next(it)
    gr_ref = next(it)
    if has_vmix:
        yb_ref = next(it)
        vf_ref = next(it)
    mu_ref = next(it)
    vecs_ref = next(it)
    wdec_ref = next(it)
    wicl_ref = next(it)
    if has_vmix:
        muy_ref = next(it)
        wvm_ref = next(it)
    tri_ref = next(it)
    bd_ref = next(it)
    out_ref = next(it)
    if not has_vmix:
        vf_out_ref = next(it)
    carry_p = next(it)
    carry_y = next(it)
    state = next(it)
    rh_s, pc_s, y_s, bonus_s, ops_s, g_s, h_s, qt_s = (next(it) for _ in range(8))

    t = pl.program_id(1)

    @pl.when(t == 0)
    def _():
        carry_p[...] = jnp.zeros_like(carry_p)
        carry_y[...] = jnp.zeros_like(carry_y)
        state[...] = jnp.zeros_like(state)

    w0 = vecs_ref[0:1, :]
    a0 = vecs_ref[1:2, :]
    k_k = vecs_ref[2:3, :]
    k_a = vecs_ref[3:4, :]
    r_k = vecs_ref[4:5, :]
    lnw = vecs_ref[5:6, :]
    lnb = vecs_ref[6:7, :]
    bd = bd_ref[...]

    def seg(x):
        xb = x.astype(BF16)
        return jnp.concatenate([_dot(xb[:, q * MXU_DIM:(q + 1) * MXU_DIM], bd)
                                for q in range(RWKV_WIDTH // MXU_DIM)], axis=1)

    bf = lambda x_: x_.astype(BF16)
    grp = min(RWKV_GROUP, ts)
    row = lax.broadcasted_iota(jnp.int32, (grp, 1), 0)

    def prepare(g):
        rs = slice(g * grp, (g + 1) * grp)

        def shifted(lo, hi):
            pc_ = p_ref[0, rs, lo:hi]
            prev = jnp.where(row == 0, carry_p[:, lo:hi], pltpu.roll(pc_, 1, 0))
            carry_p[:, lo:hi] = pc_[grp - 1:grp, :]
            return pc_ + (prev - pc_) * mu_ref[:, lo:hi]

        r = shifted(0, RWKV_WIDTH)
        yield
        k = shifted(RWKV_WIDTH, 2 * RWKV_WIDTH)
        yield
        v = shifted(2 * RWKV_WIDTH, 3 * RWKV_WIDTH)
        xl = shifted(3 * RWKV_WIDTH, SHIFT_WIDTH)
        yield
        dec = w0 + _dot(jnp.tanh(xl).astype(BF16), wdec_ref[...])
        lw = -DECAY_SCALE * _sigmoid(dec)
        a = _sigmoid(a0 + _dot(xl.astype(BF16), wicl_ref[...]))
        yield
        if has_vmix:
            yb = yb_ref[0, rs, :]
            prevy = jnp.where(row == 0, carry_y[...], pltpu.roll(yb, 1, 0))
            carry_y[...] = yb[grp - 1:grp, :]
            ys = yb + (prevy - yb) * muy_ref[...]
            v0 = vecs_ref[7:8, :]
            mix = _sigmoid(v0 + _dot(ys.astype(BF16), wvm_ref[...]))
            v = v + (vf_ref[0, rs, :] - v) * mix
        else:
            vf_out_ref[0, rs, :] = v
        yield
        kk = k * k_k
        kk = kk * lax.rsqrt(jnp.maximum(seg(kk * kk), 1e-24))
        yield
        k2 = k * (1.0 + (a - 1.0) * k_a)
        bonus_s[rs, :] = seg(r * k2 * r_k) * v
        yield
        lw_hi, lw_lo = _split(lw)
        tri = tri_ref[...]
        cl = _dot(tri, lw_hi) + _dot(tri, lw_lo)
        ce = jnp.concatenate(
            [jnp.broadcast_to(cl[(c + 1) * CHUNK - 1:(c + 1) * CHUNK, :], (CHUNK, RWKV_WIDTH))
             for c in range(grp // CHUNK)], axis=0)
        yield
        e_out = jnp.exp(-cl)
        pce = jnp.exp(ce)
        rh = r * jnp.exp(cl)
        rh_s[rs, :] = rh
        pc_s[rs, :] = pce
        ops_s[1, rs, :] = bf(rh)
        yield
        kh = k2 * e_out
        ops_s[3, rs, :] = bf(kh)
        ops_s[5, rs, :] = bf(kh * pce)
        yield
        bh = kk * a * e_out
        ops_s[2, rs, :] = bf(bh)
        ops_s[6, rs, :] = bf(bh * pce)
        yield
        ops_s[0, rs, :] = bf(kk * jnp.exp(cl - lw))
        ops_s[4, rs, :] = bf(v)

    ri = lax.broadcasted_iota(jnp.int32, (CHUNK, MXU_DIM), 0)
    lane_head = lax.broadcasted_iota(jnp.int32, (CHUNK, MXU_DIM), 1) // HEAD
    ci = lax.broadcasted_iota(jnp.int32, (CHUNK, MXU_DIM), 1) & (HEAD - 1)
    strict = ci < ri
    incl = ci <= ri
    eye_sb = jnp.where(ci == ri, 1.0, 0.0).astype(F32)
    er = lax.broadcasted_iota(jnp.int32, (MXU_DIM, MXU_DIM), 0)
    ec = lax.broadcasted_iota(jnp.int32, (MXU_DIM, MXU_DIM), 1)
    eye_bd = er == ec
    bdm = (er // HEAD) == (ec // HEAD)

    def expand(xb):
        return jnp.concatenate([xb] * QUAD, axis=0) * bd

    nchunk = ts // CHUNK
    nquad = RWKV_WIDTH // MXU_DIM
    rows = lambda c: slice(c * CHUNK, (c + 1) * CHUNK)
    lanes = lambda q: slice(q * MXU_DIM, (q + 1) * MXU_DIM)
    each = lambda f, *ls: [f(*xs) for xs in zip(*ls)]

    def transitions(g, tick):
        chains = [(c, q) for c in range(g * grp // CHUNK, (g + 1) * grp // CHUNK) for q in range(nquad)]
        op = lambda j: [ops_s[j, rows(c), lanes(q)] for c, q in chains]
        Ab, Rb, Bhb, Khb, Vb, Kpb, Bpb = (op(j) for j in range(7))
        AR = each(lambda a_, r_: jnp.concatenate([a_, r_], axis=0), Ab, Rb)
        sb = each(lambda x_, b_: _dot(x_, expand(b_), NT), AR, Bhb)
        tick()
        sk = each(lambda x_, k_: _dot(x_, expand(k_), NT), AR, Khb)
        tick()
        lab = [jnp.where(strict, x_[:CHUNK], 0.0) for x_ in sb]
        mrb = [bf(jnp.where(incl, x_[CHUNK:], 0.0)) for x_ in sb]
        lak = [bf(jnp.where(strict, x_[:CHUNK], 0.0)) for x_ in sk]
        mrk = [bf(jnp.where(incl, x_[CHUNK:], 0.0)) for x_ in sk]
        T = [eye_sb - l_ for l_ in lab]
        P = each(lambda l_: _dot(bf(l_), expand(bf(l_))), lab)
        tick()
        n = 4
        while n < CHUNK:
            tp = each(lambda t_, p_: _dot(jnp.concatenate([bf(t_), bf(p_)], axis=0), expand(bf(p_))), T, P)
            T = each(lambda t_, x_: t_ + x_[:CHUNK], T, tp)
            P = [x_[CHUNK:] for x_ in tp]
            tick()
            n *= 2
        Tb = each(lambda t_, p_: bf(t_ + _dot(bf(t_), expand(bf(p_)))), T, P)
        tick()
        lm = each(lambda l_, m_, v_: _dot(jnp.concatenate([l_, m_], axis=0), expand(v_)), lak, mrk, Vb)
        lakv = [bf(x_[:CHUNK]) for x_ in lm]
        y0 = [x_[CHUNK:] for x_ in lm]
        tick()
        Wb = each(lambda t_, a_: bf(_dot(t_, expand(a_))), Tb, Ab)
        tick()
        U0b = each(lambda t_, l_: bf(_dot(t_, expand(l_))), Tb, lakv)
        tick()
        mw = each(lambda m_, w_, u_: _dot(m_, jnp.concatenate([expand(w_), expand(u_)], axis=1)),
                  mrb, Wb, U0b)
        tick()
        wtb = each(lambda w_, b_: _dot(w_, b_, TN), Wb, Bpb)
        tick()
        hbd = each(lambda v_, u_, k_, b_: _dot(jnp.concatenate([v_, -u_], axis=0),
                                               jnp.concatenate([k_, b_], axis=0), TN),
                   Vb, U0b, Kpb, Bpb)
        tick()
        for i, (c, q) in enumerate(chains):
            qt_s[c, q] = bf(rh_s[rows(c), lanes(q)] - mw[i][:, :MXU_DIM])
            y_s[rows(c), lanes(q)] = y0[i] - mw[i][:, MXU_DIM:]
            pc = pc_s[c * CHUNK:c * CHUNK + 1, lanes(q)]
            diag = jnp.where(eye_bd, jnp.broadcast_to(pc, (MXU_DIM, MXU_DIM)), 0.0)
            g_s[c, q] = bf(diag - jnp.where(bdm, wtb[i], 0.0))
            hm = hbd[i][0:HEAD]
            for h in range(1, QUAD):
                hm = jnp.where(lane_head == h, hbd[i][h * HEAD:(h + 1) * HEAD], hm)
            h_s[c, q] = hm

    S = [state[q] for q in range(nquad)]

    def finish(g):
        for c in range(g * grp // CHUNK, (g + 1) * grp // CHUNK):
            for q in range(nquad):
                Sb = bf(S[q])
                y_s[rows(c), lanes(q)] = y_s[rows(c), lanes(q)] + _dot(qt_s[c, q], expand(Sb), NT)
                S[q] = _dot(Sb, g_s[c, q]) + h_s[c, q]
            yield
        rs = slice(g * grp, (g + 1) * grp)
        y = y_s[rs, :]
        mean = seg(y) * (1.0 / HEAD)
        d = y - mean
        yield
        var = seg(d * d) * (1.0 / HEAD)
        yn = d * lax.rsqrt(var + GN_EPS) * lnw + lnb
        yield
        out_ref[0, rs, :] = ((yn + bonus_s[rs, :]) * _silu(gr_ref[0, rs, :])).astype(BF16)

    ngroup = ts // grp
    for _ in prepare(0):
        pass
    for g in range(ngroup):
        fill = []
        if g + 1 < ngroup:
            fill.append(prepare(g + 1))
        if g >= 1:
            fill.append(finish(g - 1))
        filler = (None for gen in fill for _ in gen)
        transitions(g, lambda: next(filler, None))
        for _ in filler:
            pass
    for _ in finish(ngroup - 1):
        pass
    for q in range(nquad):
        state[q] = S[q]


def _rwkv_call(proj, v_first, lw):
    B, S, _ = proj.shape
    has_vmix = v_first is not None
    ts = min(S, RWKV_TILE)
    nP = SHIFT_WIDTH
    tok = lambda w, j: pl.BlockSpec((1, ts, w), lambda b, t: (b, t, j))
    full = lambda a: pl.BlockSpec(a.shape, lambda b, t: (0,) * a.ndim)

    args = [proj, proj]
    specs = [tok(nP, C_P // nP), tok(RWKV_WIDTH, C_GR // RWKV_WIDTH)]
    if has_vmix:
        args += [proj, v_first]
        specs += [tok(LANES, C_Y // LANES), tok(RWKV_WIDTH, 0)]
    consts = [lw["mu"], lw["vecs"], lw["wdec"], lw["wicl"]]
    if has_vmix:
        consts += [lw["muy"], lw["wvm"]]
    tid = jnp.arange(min(RWKV_GROUP, ts))
    same_chunk = (tid[:, None] // CHUNK) == (tid[None, :] // CHUNK)
    tri = (same_chunk & (tid[:, None] >= tid[None, :])).astype(BF16)
    consts += [tri, lw["bd"]]
    args += consts
    specs += [full(a) for a in consts]

    out_shape = [jax.ShapeDtypeStruct((B, S, RWKV_WIDTH), BF16)]
    out_specs = [tok(RWKV_WIDTH, 0)]
    if not has_vmix:
        out_shape.append(jax.ShapeDtypeStruct((B, S, RWKV_WIDTH), F32))
        out_specs.append(tok(RWKV_WIDTH, 0))

    big = pltpu.VMEM((ts, RWKV_WIDTH), F32)
    scratch = [pltpu.VMEM((1, nP), F32), pltpu.VMEM((1, LANES), F32),
               pltpu.VMEM((RWKV_WIDTH // MXU_DIM, HEAD, MXU_DIM), F32)] + [big] * 4
    nquad = RWKV_WIDTH // MXU_DIM
    scratch += [pltpu.VMEM((7, ts, RWKV_WIDTH), BF16),
                pltpu.VMEM((ts // CHUNK, nquad, MXU_DIM, MXU_DIM), BF16),
                pltpu.VMEM((ts // CHUNK, nquad, HEAD, MXU_DIM), F32),
                pltpu.VMEM((ts // CHUNK, nquad, CHUNK, MXU_DIM), BF16)]
    outs = pl.pallas_call(
        functools.partial(_rwkv_kernel, has_vmix=has_vmix, ts=ts),
        grid=(B, S // ts),
        in_specs=specs,
        out_specs=out_specs,
        out_shape=out_shape,
        scratch_shapes=scratch,
        compiler_params=pltpu.CompilerParams(
            dimension_semantics=("arbitrary", "arbitrary"), vmem_limit_bytes=VMEM_LIMIT),
        name="rwkv_vmix" if has_vmix else "rwkv_first",
    )(*args)
    if has_vmix:
        return outs[0], v_first
    return outs[0], outs[1]


def _rot_half(x, sina, sinb):
    half = MLA_ROPE // 2
    return pltpu.roll(x, LANES - half, 1) * sina + pltpu.roll(x, half, 1) * sinb


def _mla_prep_kernel(cq_ref, ckv_ref, yb_ref, cosq_ref, cosk_ref, sina_ref, sinb_ref,
                     gq_ref, gkv_ref, wq_ref, wkv_ref, q_ref, k_ref, v_ref, *, scale):
    cosq, cosk, sina, sinb = cosq_ref[0], cosk_ref[0], sina_ref[0], sinb_ref[0]
    cqn = _rms(cq_ref[0], gq_ref[...]).astype(BF16)
    qall = _dot(cqn, wq_ref[...])
    ckvn = _rms(ckv_ref[0], gkv_ref[...]).astype(BF16)
    kvall = _dot(ckvn, wkv_ref[...])
    yb = yb_ref[0]
    kr = yb * cosk + _rot_half(yb, sina, sinb)
    lane = lax.broadcasted_iota(jnp.int32, (1, LANES), 1)
    for h in range(HEADS):
        qh = qall[:, h * LANES:(h + 1) * LANES]
        q_ref[0, h] = ((qh * cosq + _rot_half(qh, sina, sinb)) * scale).astype(BF16)
        k_ref[0, h] = (kvall[:, h * LANES:(h + 1) * LANES] + kr).astype(BF16)
        ones_lane = jnp.where(lane == (HEAD if h % 2 == 0 else 0), 1.0, 0.0)
        v_ref[0, h] = (kvall[:, (HEADS + h) * LANES:(HEADS + h + 1) * LANES] + ones_lane).astype(BF16)


def _mla_prep_call(proj, tabs, gq, gkv, wq_all, wkv_all, l):
    B, S, _ = proj.shape
    tm = min(S, STREAM_TILE)
    tok = lambda w, j: pl.BlockSpec((1, tm, w), lambda b, t: (b, t, j))
    full = lambda a: pl.BlockSpec(a.shape, lambda b, t: (0,) * a.ndim)
    slab =jax.ShapeDtypeStruct((B, HEADS, S, LANES), BF16)
    slab_spec = pl.BlockSpec((1, HEADS, tm, LANES), lambda b, t: (b, 0, t, 0))
    scale = float(HEAD + MLA_ROPE) ** -0.5 * 1.4426950408889634
    return pl.pallas_call(
        functools.partial(_mla_prep_kernel, scale=scale),
        grid=(B, S // tm),
        in_specs=[tok(Q_LORA, C_Q // Q_LORA), tok(KV_LORA, C_KV // KV_LORA), tok(LANES, C_Y // LANES),
                  tok(LANES, 0), tok(LANES, 0), tok(LANES, 0), tok(LANES, 0),
                  full(gq), full(gkv), _layer_spec(wq_all, l), _layer_spec(wkv_all, l)],
        out_specs=[slab_spec, slab_spec, slab_spec],
        out_shape=[slab, slab, slab],
        compiler_params=pltpu.CompilerParams(
            dimension_semantics=("arbitrary", "arbitrary"), vmem_limit_bytes=VMEM_LIMIT),
        name="mla_prep",
    )(proj, proj, proj, *tabs, gq, gkv, wq_all, wkv_all)


ATT_HEADS = 8
ATT_BLOCK = 512
ATT_BATCH = 2


def _attn_kernel(qi_ref, ki_ref, q_ref, k_ref, v_ref, g_ref, o_ref, m_s, acc_s, *, blk, nb):
    step = pl.program_id(2)
    qi = qi_ref[step]
    ki = ki_ref[step]
    units = [(b_, h_) for b_ in range(nb) for h_ in range(ATT_HEADS)]

    @pl.when(ki == 0)
    def _():
        m_s[...] = jnp.full_like(m_s, -jnp.inf)
        acc_s[...] = jnp.zeros_like(acc_s)

    half = blk // 2

    def update(diagonal):
        if diagonal:
            dead0 = (lax.broadcasted_iota(jnp.int32, (half, blk), 0)
                     > lax.broadcasted_iota(jnp.int32, (half, blk), 1))
            dead1 = (lax.broadcasted_iota(jnp.int32, (half, half), 0)
                     > lax.broadcasted_iota(jnp.int32, (half, half), 1))

        def score(u_):
            b_, h_ = u_
            if not diagonal:
                return (_dot(k_ref[b_, h_], q_ref[b_, h_], NT),)
            return (_dot(k_ref[b_, h_, 0:half, :], q_ref[b_, h_], NT),
                    _dot(k_ref[b_, h_, half:blk, :], q_ref[b_, h_, half:blk, :], NT))

        def accumulate(i_, u_, pts_, alpha_):
            b_, h_ = u_
            if not diagonal:
                acc_s[i_] = alpha_ * acc_s[i_] + _dot(v_ref[b_, h_], pts_[0], TN)
            else:
                pv1 = _dot(v_ref[b_, h_, half:blk, :], pts_[1], TN)
                pv = _dot(v_ref[b_, h_, 0:half, :], pts_[0], TN)
                pv = jnp.concatenate([pv[:, 0:half], pv[:, half:blk] + pv1], axis=1)
                acc_s[i_] = alpha_ * acc_s[i_] + pv

        st_next = score(units[0])
        pending = None
        for i, u in enumerate(units):
            sts = st_next
            if i + 1 < len(units):
                st_next = score(units[i + 1])
            if pending is not None:
                accumulate(*pending)
            m_prev = m_s[i]
            if not diagonal:
                m_new = jnp.maximum(m_prev, jnp.max(sts[0], axis=0, keepdims=True))
                pts = (jnp.exp2(sts[0] - m_new).astype(BF16),)
            else:
                st0 = jnp.where(dead0, -jnp.inf, sts[0])
                st1 = jnp.where(dead1, -jnp.inf, sts[1])
                mx0 = jnp.max(st0, axis=0, keepdims=True)
                mx1 = jnp.max(st1, axis=0, keepdims=True)
                m_cur = jnp.concatenate([mx0[:, 0:half], jnp.maximum(mx0[:, half:blk], mx1)], axis=1)
                m_new = jnp.maximum(m_prev, m_cur)
                pts = (jnp.exp2(st0 - m_new).astype(BF16),
                       jnp.exp2(st1 - m_new[:, half:blk]).astype(BF16))
            alpha = jnp.exp2(m_prev - m_new)
            m_s[i] = m_new
            pending = (i, u, pts, alpha)
        accumulate(*pending)

    @pl.when(ki < qi)
    def _():
        update(False)

    @pl.when(ki == qi)
    def _():
        update(True)
        upper = lax.broadcasted_iota(jnp.int32, (LANES, blk), 0) < HEAD
        for b_ in range(nb):
            g = _silu(g_ref[b_])
            for j in range(ATT_HEADS // 2):
                even, odd = acc_s[b_ * ATT_HEADS + 2 * j], acc_s[b_ * ATT_HEADS + 2 * j + 1]
                ot = jnp.where(upper, even / even[HEAD:HEAD + 1, :], odd / odd[0:1, :])
                o_ref[b_, :, j * LANES:(j + 1) * LANES] = (
                    ot.T * g[:, j * LANES:(j + 1) * LANES]).astype(BF16)


def _attn_call(q, k, v, proj):
    B, H, S, _ = q.shape
    blk = min(S, ATT_BLOCK)
    nb = ATT_BATCH if B % ATT_BATCH == 0 else 1
    n = S // blk
    pairs = [(i, j) for i in range(n) for j in range(i + 1)]
    qi_tab = jnp.array([i for i, _ in pairs], jnp.int32)
    ki_tab = jnp.array([j for _, j in pairs], jnp.int32)
    wout = ATT_HEADS * HEAD
    gm0 = C_GM // wout
    qspec = pl.BlockSpec((nb, ATT_HEADS, blk, LANES), lambda b, g, s, qt, kt: (b, g, qt[s], 0))
    kspec = pl.BlockSpec((nb, ATT_HEADS, blk, LANES), lambda b, g, s, qt, kt: (b, g, kt[s], 0))
    grid_spec = pltpu.PrefetchScalarGridSpec(
        num_scalar_prefetch=2,
        grid=(B // nb, H // ATT_HEADS, len(pairs)),
        in_specs=[qspec, kspec, kspec,
                  pl.BlockSpec((nb, blk, wout), lambda b, g, s, qt, kt: (b, qt[s], gm0 + g))],
        out_specs=pl.BlockSpec((nb, blk, wout), lambda b, g, s, qt, kt: (b, qt[s], g)),
        scratch_shapes=[pltpu.VMEM((nb * ATT_HEADS, 1, blk), F32),
                        pltpu.VMEM((nb * ATT_HEADS, LANES, blk), F32)],
    )
    return pl.pallas_call(
        functools.partial(_attn_kernel, blk=blk, nb=nb),
        grid_spec=grid_spec,
        out_shape=jax.ShapeDtypeStruct((B, S, H * HEAD), BF16),
        compiler_params=pltpu.CompilerParams(
            dimension_semantics=("arbitrary",) * 3, vmem_limit_bytes=VMEM_LIMIT),
        name="mla_attn",
    )(qi_tab, ki_tab, q, k, v, proj)


def _outproj_kernel(yr_ref, ym_ref, x_ref, gate_ref, w_ref, fg_ref, o_ref, *, final):
    y = _dot(yr_ref[0], w_ref[0:RWKV_WIDTH, :]) + _dot(ym_ref[0], w_ref[RWKV_WIDTH:, :])
    xn = x_ref[0] + gate_ref[0] * y
    if final:
        xn = _rms(xn, fg_ref[...])
    o_ref[0] = xn


def _outproj_call(yr, ym, x, gate, w_all, l, fg, final):
    B, S, D = x.shape
    tm = min(S, STREAM_TILE)
    tok = lambda w_: pl.BlockSpec((1, tm, w_), lambda b, t: (b, t, 0))
    return pl.pallas_call(
        functools.partial(_outproj_kernel, final=final),
        grid=(B, S // tm),
        in_specs=[tok(RWKV_WIDTH), tok(RWKV_WIDTH), tok(D),
                  pl.BlockSpec((1, 1, D), lambda b, t: (b, 0, 0)),
                  _layer_spec(w_all, l),
                  pl.BlockSpec((1, D), lambda b, t: (0, 0))],
        out_specs=tok(D),
        out_shape=jax.ShapeDtypeStruct((B, S, D), F32),
        compiler_params=pltpu.CompilerParams(
            dimension_semantics=("arbitrary", "arbitrary"), vmem_limit_bytes=VMEM_LIMIT),
        name="outproj_final" if final else "outproj",
    )(yr, ym, x, gate.reshape(B, 1, D), w_all, fg.reshape(1, D))


def _pack_w_in(w_in, w_vmix_down):
    L, D, _ = w_in.shape
    z = lambda n: jnp.zeros((L, D, n), F32)
    o_kr = SHIFT_WIDTH + RWKV_WIDTH + Q_LORA + KV_LORA
    o_cq = SHIFT_WIDTH + RWKV_WIDTH
    o_ckv = o_cq + Q_LORA
    o_gm = o_kr + MLA_ROPE
    vm = jnp.concatenate([jnp.zeros((1, D, VMIX_LORA), F32), w_vmix_down], axis=0)
    yblk = jnp.concatenate([vm, z(KR_LANE - VMIX_LORA), w_in[:, :, o_kr:o_gm],
                            z(LANES - KR_LANE - MLA_ROPE)], axis=2)
    cols = [w_in[:, :, :SHIFT_WIDTH], yblk, w_in[:, :, o_ckv:o_kr],
            w_in[:, :, SHIFT_WIDTH:o_cq], w_in[:, :, o_gm:IN_WIDTH], w_in[:, :, o_cq:o_ckv]]
    return jnp.concatenate(cols, axis=2).astype(BF16)


def _pack_wq(w_uq):
    L = w_uq.shape[0]
    w = w_uq.reshape(L, Q_LORA, HEADS, HEAD + MLA_ROPE)
    w = jnp.pad(w, ((0, 0), (0, 0), (0, 0), (0, LANES - HEAD - MLA_ROPE)))
    return w.reshape(L, Q_LORA, HEADS * LANES).astype(BF16)


def _pack_wkv(w_ukv):
    L = w_ukv.shape[0]
    w = w_ukv.reshape(L, KV_LORA, HEADS, 2 * HEAD)
    wk = jnp.pad(w[..., :HEAD], ((0, 0), (0, 0), (0, 0), (0, LANES - HEAD)))
    wv = w[..., HEAD:]
    zero = jnp.zeros_like(wv)
    even = jnp.concatenate([wv, zero], axis=-1)
    odd = jnp.concatenate([zero, wv], axis=-1)
    is_odd = (jnp.arange(HEADS) % 2 == 1)[None, None, :, None]
    wv2 = jnp.where(is_odd, odd, even)
    return jnp.concatenate([wk.reshape(L, KV_LORA, -1), wv2.reshape(L, KV_LORA, -1)], axis=2).astype(BF16)


def _pad_rows(w, lo, total):
    return jnp.pad(w, ((lo, total - lo - w.shape[0]), (0, 0)))


def kernel(x, c, positions, norm_g, w_ada, b_ada, w_in, w_vmix_down, mu_shift, mu_vmix, w0, w_decay_up, a0, w_iclr_up, v0, w_vmix_up, k_k, k_a, r_k, lnx_w, lnx_b, q_norm_g, kv_norm_g, w_uq, w_ukv, w_out, final_g):
    B, S, D = x.shape
    L = w_in.shape[0]
    assert D == D_MODEL and S % CHUNK == 0

    mod = _mod_call(c, w_ada, b_ada)
    tabs = _rope_call(positions)

    qid = jnp.arange(MXU_DIM) // HEAD
    bd = (qid[:, None] == qid[None, :]).astype(BF16)

    w_in_all = _pack_w_in(w_in, w_vmix_down)
    wq_all, wkv_all = _pack_wq(w_uq), _pack_wkv(w_ukv)
    w_out_all = w_out.astype(BF16)

    v_first = None
    for l in range(L):
        shift, scale, gate = mod[l, :, :D], mod[l, :, D:2 * D], mod[l, :, 2 * D:]
        proj = _inproj_call(x, norm_g[l], scale, shift, w_in_all, l)

        vecs = jnp.stack([w0[l], a0[l], k_k[l], k_a[l], r_k[l].reshape(-1), lnx_w[l], lnx_b[l],
                          v0[l - 1] if l > 0 else jnp.zeros((RWKV_WIDTH,), F32)])
        lw = {
            "mu": mu_shift[l].reshape(1, SHIFT_WIDTH),
            "vecs": vecs,
            "wdec": _pad_rows(w_decay_up[l], 0, LANES).astype(BF16),
            "wicl": _pad_rows(w_iclr_up[l], DECAY_LORA, LANES).astype(BF16),
            "bd": bd,
        }
        if l > 0:
            lw["muy"] = jnp.pad(mu_vmix[l - 1], (0, LANES - VMIX_LORA)).reshape(1, LANES)
            lw["wvm"] = _pad_rows(w_vmix_up[l - 1], 0, LANES).astype(BF16)
        y_rwkv, v_first = _rwkv_call(proj, v_first, lw)

        q, k, v = _mla_prep_call(proj, tabs, q_norm_g[l].reshape(1, Q_LORA),
                                 kv_norm_g[l].reshape(1, KV_LORA), wq_all, wkv_all, l)
        y_mla = _attn_call(q, k, v, proj)

        x = _outproj_call(y_rwkv, y_mla, x, gate, w_out_all, l, final_g, final=(l == L - 1))
    return x
```
